```python
import jax
import jax.numpy as jnp
from jax import lax
import numpy as np

D_MODEL = 1024
BATCH = 8
SEQ = 8192
DEPTH = 2

GRID_W = 64
CTX_LEN = 256
N_MIXERS = 2
D_FF = 2816
CONV_WIDTH = 3
HG_HEADS = 8
HG_DK = D_MODEL // HG_HEADS
HG_CHUNK = 64
N_CONV_LAYERS = (DEPTH + 1) // 2
N_HGRN_LAYERS = DEPTH // 2
N_MOD = 9
EPS = 1e-6

kernel_name = 'hybrid_shortconv_hgrn2_macaron_flow'


def rms_norm(x, g):
    xf = x.astype(jnp.float32)
    y = xf * lax.rsqrt(jnp.mean(xf * xf, axis=-1, keepdims=True) + EPS)
    return (y * g.astype(jnp.float32)).astype(x.dtype)


def ada_mods(cond, w, b):
    m = jax.nn.silu(cond) @ w + b
    return jnp.split(m[..., None, :], N_MOD, axis=-1)


def modulate(x, shift, scale):
    return x * (1.0 + scale) + shift


def swiglu(x, w_gu, w_down):
    gate, up = jnp.split(x @ w_gu, 2, axis=-1)
    return (jax.nn.silu(gate) * up) @ w_down


def ffn_half(h, shift, scale, gate, g, w_gu, w_down):
    return h + 0.5 * gate * swiglu(modulate(rms_norm(h, g), shift, scale), w_gu, w_down)


def conv3(z, w, axis):
    n = z.shape[axis]
    pad = [(0, 0)] * z.ndim
    pad[axis] = (1, 1)
    zp = jnp.pad(z, pad)
    return sum(lax.slice_in_dim(zp, j, j + n, axis=axis) * w[j] for j in range(CONV_WIDTH))


def grid_conv(z, w, rows):
    b, s, d = z.shape
    half = d // 2
    zg = z.reshape(b, rows, GRID_W, d)
    yh = conv3(zg[..., :half], w[:, :half], axis=2)
    yv = conv3(zg[..., half:], w[:, half:], axis=1)
    return jnp.concatenate([yh, yv], axis=-1).reshape(b, s, d)


def shortconv_mixer(a, w_in, conv_w, w_out, rows):
    bg, cg, v = jnp.split(a @ w_in, 3, axis=-1)
    z = cg * v
    zc = conv3(z, conv_w, axis=1) if rows is None else grid_conv(z, conv_w, rows)
    return (bg * zc) @ w_out


def hgrn_lower_bounds(lb_logits):
    p = jax.nn.softmax(lb_logits.astype(jnp.float32), axis=0)
    return jnp.cumsum(p, axis=0) - p[0]


def to_heads(t):
    return t.astype(jnp.float32).reshape(t.shape[0], t.shape[1], HG_HEADS, HG_DK)


def hgrn_gates(f_logit, lb):
    f = lb + (1.0 - lb) * jax.nn.sigmoid(f_logit.astype(jnp.float32))
    return to_heads(1.0 - f), to_heads(jnp.log(f))


def chunk_gla_scan(q, k, v, logf):
    b, l, h, dk = q.shape
    dv = v.shape[-1]
    n = l // HG_CHUNK

    def to_chunks(t):
        return jnp.moveaxis(t.reshape(b, n, HG_CHUNK, h, t.shape[-1]), 1, 0)

    mask = jnp.tril(jnp.ones((HG_CHUNK, HG_CHUNK), dtype=bool))

    def step(s, inp):
        qc, kc, vc, gc = inp
        g_cum = jnp.cumsum(gc, axis=1)
        q_t = qc * jnp.exp(g_cum)
        k_t = kc * jnp.exp(-g_cum)
        att = jnp.where(mask, jnp.einsum('bthd,bshd->bhts', q_t, k_t), 0.0)
        o = jnp.einsum('bhts,bshe->bthe', att, vc) + jnp.einsum('bthd,bhde->bthe', q_t, s)
        g_last = g_cum[:, -1]
        k_d = kc * jnp.exp(g_last[:, None] - g_cum)
        s = jnp.exp(g_last)[..., None] * s + jnp.einsum('bshd,bshe->bhde', k_d, vc)
        return s, o

    s0 = jnp.zeros((b, h, dk, dv), jnp.float32)
    _, o = lax.scan(step, s0, (to_chunks(q), to_chunks(k), to_chunks(v), to_chunks(logf)))
    return jnp.moveaxis(o, 0, 1).reshape(b, l, h, dv)


def prefix_reverse(t, lc):
    return jnp.concatenate([jnp.flip(t[:, :lc], axis=1), jnp.flip(t[:, lc:], axis=1)], axis=1)


def hgrn2_mixer(a_ctx, a_lat, w_in, lb_fwd, lb_bwd, gnorm_g, w_out, need_ctx):
    lc = a_ctx.shape[1]
    a = jnp.concatenate([a_ctx, a_lat], axis=1)
    q, i, f_fw, f_bw, og = jnp.split(a @ w_in, 5, axis=-1)
    q = to_heads(jax.nn.silu(q))
    i = to_heads(i)
    k_fw, lf_fw = hgrn_gates(f_fw, lb_fwd)
    k_bw, lf_bw = hgrn_gates(f_bw, lb_bwd)
    o_fw = chunk_gla_scan(q, k_fw, i, lf_fw)
    o_bw = prefix_reverse(chunk_gla_scan(prefix_reverse(q, lc), prefix_reverse(k_bw, lc),
                                         prefix_reverse(i, lc), prefix_reverse(lf_bw, lc)), lc)
    o = o_fw + o_bw
    if not need_ctx:
        o, og = o[:, lc:], og[:, lc:]
    o = o * lax.rsqrt(jnp.mean(o * o, axis=-1, keepdims=True) + EPS)
    o = o * gnorm_g.astype(jnp.float32).reshape(HG_HEADS, HG_DK)
    o = o.reshape(o.shape[0], o.shape[1], D_MODEL).astype(a.dtype)
    y = (o * jax.nn.silu(og)) @ w_out
    if need_ctx:
        return y[:, :lc], y[:, lc:]
    return None, y


def setup_inputs(seed: int = 0) -> dict:
    key = jax.random.key(seed)
    ks = jax.random.split(key, 17)
    d, f = D_MODEL, D_FF
    na, nb = N_CONV_LAYERS, N_HGRN_LAYERS

    def nrm(k, shape, scale=1.0):
        return scale * jax.random.normal(k, shape, jnp.float32)

    return {
        'x': nrm(ks[0], (BATCH, SEQ, d)),
        'c': nrm(ks[1], (BATCH, d)),
        'ctx': nrm(ks[2], (BATCH, CTX_LEN, d)),
        'c_ctx': nrm(ks[3], (d,)),
        'ada_w': nrm(ks[4], (DEPTH, d, N_MOD * d), 0.5 * d ** -0.5),
        'ada_b': nrm(ks[5], (DEPTH, N_MOD * d), 0.02),
        'norm_g': 1.0 + nrm(ks[6], (DEPTH, 3, d), 0.1),
        'ffn_w_gu': nrm(ks[7], (DEPTH, 2, d, 2 * f), d ** -0.5),
        'ffn_w_down': nrm(ks[8], (DEPTH, 2, f, d), f ** -0.5),
        'conv_w_in': nrm(ks[9], (na, d, 3 * d), d ** -0.5),
        'conv_w': nrm(ks[10], (na, CONV_WIDTH, d), CONV_WIDTH ** -0.5),
        'conv_w_out': nrm(ks[11], (na, d, d), d ** -0.5),
        'hg_w_in': nrm(ks[12], (nb, d, 5 * d), d ** -0.5),
        'hg_lb_logits': nrm(ks[13], (DEPTH, 2, d), 0.1),
        'hg_gnorm_g': 1.0 + nrm(ks[14], (nb, d), 0.1),
        'hg_w_out': nrm(ks[15], (nb, d, d), d ** -0.5),
        'final_norm_g': 1.0 + nrm(ks[16], (d,), 0.1),
    }


def reference(x, c, ctx, c_ctx, ada_w, ada_b, norm_g, ffn_w_gu, ffn_w_down, conv_w_in, conv_w,
              conv_w_out, hg_w_in, hg_lb_logits, hg_gnorm_g, hg_w_out, final_norm_g):
    rows = x.shape[1] // GRID_W
    lbs = hgrn_lower_bounds(hg_lb_logits)
    h, hc = x, ctx
    for layer in range(DEPTH):
        kind = layer % N_MIXERS
        j = layer // N_MIXERS
        last = layer == DEPTH - 1
        ctx_to_mixer = (kind == 1) or (not last)
        m = ada_mods(c, ada_w[layer], ada_b[layer])
        mc = ada_mods(c_ctx, ada_w[layer], ada_b[layer])
        g1, g2, g3 = norm_g[layer, 0], norm_g[layer, 1], norm_g[layer, 2]
        w_gu, w_dn = ffn_w_gu[layer], ffn_w_down[layer]
        h = ffn_half(h, m[0], m[1], m[2], g1, w_gu[0], w_dn[0])
        if ctx_to_mixer:
            hc = ffn_half(hc, mc[0], mc[1], mc[2], g1, w_gu[0], w_dn[0])
        a = modulate(rms_norm(h, g2), m[3], m[4])
        y_ctx = None
        if kind == 0:
            y = shortconv_mixer(a, conv_w_in[j], conv_w[j], conv_w_out[j], rows)
            if not last:
                ac = modulate(rms_norm(hc, g2), mc[3], mc[4])
                y_ctx = shortconv_mixer(ac, conv_w_in[j], conv_w[j], conv_w_out[j], None)
        else:
            ac = modulate(rms_norm(hc, g2), mc[3], mc[4])
            y_ctx, y = hgrn2_mixer(ac, a, hg_w_in[j], lbs[layer, 0], lbs[layer, 1],
                                   hg_gnorm_g[j], hg_w_out[j], not last)
        h = h + m[5] * y
        h = ffn_half(h, m[6], m[7], m[8], g3, w_gu[1], w_dn[1])
        if not last:
            hc = hc + mc[5] * y_ctx
            hc = ffn_half(hc, mc[6], mc[7], mc[8], g3, w_gu[1], w_dn[1])
    return rms_norm(h, final_norm_g)
```

```python
import functools

import jax
import jax.numpy as jnp
from jax import lax
from jax.experimental import pallas as pl
from jax.experimental.pallas import tpu as pltpu

EPS = 1e-6
GRID_W = 64
N_MOD = 9
HG_HEADS = 8
HG_CHUNK = 64
COND_ROWS = 16

FFN_CHUNK = 256
TM_FFN = 256
TM_CONV = 256
TM_SCAN = 256
VMEM_LIMIT = 56 * 1024 * 1024

_F32 = jnp.float32
_BF16 = jnp.bfloat16


def _params():
    return pltpu.CompilerParams(dimension_semantics=("arbitrary", "arbitrary"),
                                vmem_limit_bytes=VMEM_LIMIT)


def _const_spec(shape, index_map):
    return pl.BlockSpec(shape, index_map, pipeline_mode=pl.Buffered(1))


def _rms_norm(x, g):
    return x * lax.rsqrt(jnp.mean(x * x, axis=-1, keepdims=True) + EPS) * g


def _modulated(x, g, m, idx):
    return _rms_norm(x, g) * (1.0 + m[idx + 1:idx + 2, :]) + m[idx:idx + 1, :]


def _silu(x):
    return x * jax.nn.sigmoid(x)


def _dot(a, b):
    return jnp.dot(a, b, preferred_element_type=_F32)


def _ada_kernel(c_ref, w_ref, b_ref, o_ref):
    s = _silu(c_ref[...]).astype(_BF16)
    o_ref[...] = _dot(s, w_ref[...].astype(_BF16)) + b_ref[...]


def _ada_mods(cc, ada_w, ada_b):
    depth, d, nd = ada_w.shape
    out = pl.pallas_call(
        _ada_kernel,
        grid=(depth, nd // d),
        in_specs=[
            pl.BlockSpec((COND_ROWS, d), lambda l, k: (0, 0)),
            pl.BlockSpec((None, d, d), lambda l, k: (l, 0, k)),
            pl.BlockSpec((None, 1, d), lambda l, k: (l, 0, k)),
        ],
        out_specs=pl.BlockSpec((None, COND_ROWS, d), lambda l, k: (l, 0, k)),
        out_shape=jax.ShapeDtypeStruct((depth, COND_ROWS, nd), _F32),
        compiler_params=_params(),
        name="ada_mods",
    )(cc, ada_w, ada_b.reshape(depth, 1, nd))
    return out.reshape(depth, COND_ROWS, nd // d, d)


def _ffn_kernel(*refs, idx, n_chunk, final):
    if final:
        h_ref, m_ref, g_ref, wgu_ref, wdn_ref, fg_ref, o_ref = refs
    else:
        h_ref, m_ref, g_ref, wgu_ref, wdn_ref, o_ref = refs
    x = h_ref[...]
    m = m_ref[...]
    a = _modulated(x, g_ref[...], m, idx).astype(_BF16)
    fc = wdn_ref.shape[1]
    acc = jnp.zeros(x.shape, _F32)
    for c in range(n_chunk):
        gu = _dot(a, wgu_ref[c])
        act = (_silu(gu[:, :fc]) * gu[:, fc:]).astype(_BF16)
        acc = acc + _dot(act, wdn_ref[c])
    out = x + 0.5 * m[idx + 2:idx + 3, :] * acc
    if final:
        out = _rms_norm(out, fg_ref[...])
    o_ref[...] = out


def _ffn_half(h, mods, row_of_batch, g, wgu, wdn, idx, tm, final_g=None):
    b, s, d = h.shape
    n_chunk, _, fc2 = wgu.shape
    in_specs = [
        pl.BlockSpec((None, tm, d), lambda i, j: (i, j, 0)),
        pl.BlockSpec((None, N_MOD, d), lambda i, j: (row_of_batch(i), 0, 0)),
        _const_spec((1, d), lambda i, j: (0, 0)),
        _const_spec((n_chunk, d, fc2), lambda i, j: (0, 0, 0)),
        _const_spec((n_chunk, fc2 // 2, d), lambda i, j: (0, 0, 0)),
    ]
    args = [h, mods, g.reshape(1, d), wgu, wdn]
    if final_g is not None:
        in_specs.append(_const_spec((1, d), lambda i, j: (0, 0)))
        args.append(final_g.reshape(1, d))
    return pl.pallas_call(
        functools.partial(_ffn_kernel, idx=idx, n_chunk=n_chunk, final=final_g is not None),
        grid=(b, s // tm),
        in_specs=in_specs,
        out_specs=pl.BlockSpec((None, tm, d), lambda i, j: (i, j, 0)),
        out_shape=jax.ShapeDtypeStruct(h.shape, _F32),
        compiler_params=_params(),
        name="ffn_half",
    )(*args)


def _conv_kernel(*refs, period, grid_rows):
    if grid_rows:
        h_ref, hp_ref, hn_ref, m_ref, g_ref, win_ref, cw_ref, wout_ref, o_ref = refs
    else:
        h_ref, m_ref, g_ref, win_ref, cw_ref, wout_ref, o_ref = refs
    x = h_ref[...]
    tm, d = x.shape
    half = d // 2
    m = m_ref[...]
    g = g_ref[...]
    cw = cw_ref[...]
    a = _modulated(x, g, m, 3).astype(_BF16)
    bg = _dot(a, win_ref[:, :d])
    z = _dot(a, win_ref[:, d:2 * d]) * _dot(a, win_ref[:, 2 * d:])

    def seq_taps(zz, w):
        pos = lax.broadcasted_iota(jnp.int32, zz.shape, 0) % period
        left = jnp.where(pos == 0, 0.0, pltpu.roll(zz, 1, 0))
        right = jnp.where(pos == period - 1, 0.0, pltpu.roll(zz, zz.shape[0] - 1, 0))
        return left * w[0:1] + zz * w[1:2] + right * w[2:3]

    if grid_rows:
        j = pl.program_id(1)
        last = pl.num_programs(1) - 1

        def halo(ref, keep):
            ah = _modulated(ref[...], g, m, 3).astype(_BF16)
            zh = _dot(ah, win_ref[:, d + half:2 * d]) * _dot(ah, win_ref[:, 2 * d + half:])
            return jnp.where(keep, zh, 0.0)

        zv = z[:, half:]
        up = jnp.concatenate([halo(hp_ref, j > 0), zv[:tm - GRID_W]], axis=0)
        dn = jnp.concatenate([zv[GRID_W:], halo(hn_ref, j < last)], axis=0)
        wv = cw[:, half:]
        zc = jnp.concatenate(
            [seq_taps(z[:, :half], cw[:, :half]), up * wv[0:1] + zv * wv[1:2] + dn * wv[2:3]], axis=1)
    else:
        zc = seq_taps(z, cw)
    y = _dot((bg * zc).astype(_BF16), wout_ref[...])
    o_ref[...] = x + m[5:6, :] * y


def _conv_mixer(h, mods, row_of_batch, g, win, cw, wout, tm, grid_rows):
    b, s, d = h.shape
    spec_main = pl.BlockSpec((None, tm, d), lambda i, j: (i, j, 0))
    in_specs = [spec_main]
    args = [h]
    if grid_rows:
        r = tm // GRID_W
        n_rows = s // GRID_W
        in_specs += [
            pl.BlockSpec((None, GRID_W, d), lambda i, j: (i, jnp.maximum(j * r - 1, 0), 0)),
            pl.BlockSpec((None, GRID_W, d), lambda i, j: (i, jnp.minimum((j + 1) * r, n_rows - 1), 0)),
        ]
        args += [h, h]
    in_specs += [
        pl.BlockSpec((None, N_MOD, d), lambda i, j: (row_of_batch(i), 0, 0)),
        _const_spec((1, d), lambda i, j: (0, 0)),
        _const_spec((d, 3 * d), lambda i, j: (0, 0)),
        _const_spec((3, d), lambda i, j: (0, 0)),
        _const_spec((d, d), lambda i, j: (0, 0)),
    ]
    args += [mods, g.reshape(1, d), win, cw, wout]
    return pl.pallas_call(
        functools.partial(_conv_kernel, period=GRID_W if grid_rows else s, grid_rows=grid_rows),
        grid=(b, s // tm),
        in_specs=in_specs,
        out_specs=spec_main,
        out_shape=jax.ShapeDtypeStruct(h.shape, _F32),
        compiler_params=_params(),
        name="conv_mixer",
    )(*args)


def _chunk_cumsum(x, reverse):
    n = x.shape[0]
    pos = lax.broadcasted_iota(jnp.int32, x.shape, 0) % HG_CHUNK
    step = 1
    while step < HG_CHUNK:
        if reverse:
            x = x + jnp.where(pos < HG_CHUNK - step, pltpu.roll(x, n - step, 0), 0.0)
        else:
            x = x + jnp.where(pos >= step, pltpu.roll(x, step, 0), 0.0)
        step *= 2
    return x


def _lower_bound(lb_ref, layer, direction):
    depth = lb_ref.shape[0]
    rows = [lb_ref[l][direction:direction + 1, :] for l in range(depth)]
    mx = functools.reduce(jnp.maximum, rows)
    ex = [jnp.exp(r - mx) for r in rows]
    den = functools.reduce(lambda u, v: u + v, ex)
    p = [e / den for e in ex]
    return functools.reduce(lambda u, v: u + v, p[:layer + 1]) - p[0]


def _scan_tile(q, v, f_logit, lb, s_ref, qt_ref, kt_ref, kd_ref, v_ref, o_ref, reverse):
    tm, d = q.shape
    dk = d // HG_HEADS
    n_chunk = tm // HG_CHUNK
    f = lb + (1.0 - lb) * jax.nn.sigmoid(f_logit)
    k = 1.0 - f
    gc = _chunk_cumsum(jnp.log(f), reverse)
    qt_ref[...] = (q * jnp.exp(gc)).astype(_BF16)
    kt_ref[...] = (k * jnp.exp(-gc)).astype(_BF16)
    v_ref[...] = v.astype(_BF16)
    edge = 0 if reverse else HG_CHUNK - 1
    g_last = [gc[c * HG_CHUNK + edge:c * HG_CHUNK + edge + 1, :] for c in range(n_chunk)]
    for c in range(n_chunk):
        rows = slice(c * HG_CHUNK, (c + 1) * HG_CHUNK)
        kd_ref[rows, :] = (k[rows] * jnp.exp(g_last[c] - gc[rows])).astype(_BF16)
    ti = lax.broadcasted_iota(jnp.int32, (HG_CHUNK, HG_CHUNK), 0)
    si = lax.broadcasted_iota(jnp.int32, (HG_CHUNK, HG_CHUNK), 1)
    mask = (ti <= si) if reverse else (ti >= si)
    nt = (((1,), (1,)), ((), ()))
    tn = (((0,), (0,)), ((), ()))
    for ci in range(n_chunk):
        c = n_chunk - 1 - ci if reverse else ci
        rows = slice(c * HG_CHUNK, (c + 1) * HG_CHUNK)
        decay = jnp.exp(g_last[c])
        for h in range(HG_HEADS):
            cols = slice(h * dk, (h + 1) * dk)
            qt = qt_ref[rows, cols]
            vv = v_ref[rows, cols]
            st = s_ref[h]
            att = lax.dot_general(qt, kt_ref[rows, cols], nt, preferred_element_type=_F32)
            att = jnp.where(mask, att, 0.0).astype(_BF16)
            o_ref[rows, cols] = _dot(att, vv) + lax.dot_general(
                qt, st.astype(_BF16), nt, preferred_element_type=_F32)
            s_ref[h] = st * decay[:, cols] + lax.dot_general(
                vv, kd_ref[rows, cols], tn, preferred_element_type=_F32)


def _hgrn_kernel(*refs, layer, reverse):
    if reverse:
        (h_ref, hc_ref, m_ref, g_ref, wq_ref, wi_ref, wf_ref, wog_ref, wout_ref, lb_ref, gn_ref, ofw_ref,
         out_ref, s_ref, qt_ref, kt_ref, kd_ref, v_ref, o_ref) = refs
    else:
        (h_ref, hc_ref, m_ref, g_ref, wq_ref, wi_ref, wf_ref, lb_ref,
         out_ref, s_ref, qt_ref, kt_ref, kd_ref, v_ref, o_ref) = refs
    j = pl.program_id(1)

    @pl.when(j == 0)
    def _():
        s_ref[...] = jnp.zeros(s_ref.shape, _F32)

    x = jnp.where(j == 0, hc_ref[...], h_ref[...])
    m = m_ref[...]
    a = _modulated(x, g_ref[...], m, 3).astype(_BF16)
    q = _silu(_dot(a, wq_ref[...]))
    v = _dot(a, wi_ref[...])
    lb = _lower_bound(lb_ref, layer, 1 if reverse else 0)
    _scan_tile(q, v, _dot(a, wf_ref[...]), lb, s_ref, qt_ref, kt_ref, kd_ref, v_ref, o_ref, reverse)
    if not reverse:
        out_ref[...] = o_ref[...]
        return

    @pl.when(j > 0)
    def _():
        d = x.shape[1]
        dk = d // HG_HEADS
        o = ofw_ref[...] + o_ref[...]
        gn = gn_ref[...]
        normed = [_rms_norm(o[:, h * dk:(h + 1) * dk], gn[:, h * dk:(h + 1) * dk]) for h in range(HG_HEADS)]
        gated = (jnp.concatenate(normed, axis=1) * _silu(_dot(a, wog_ref[...]))).astype(_BF16)
        out_ref[...] = x + m[5:6, :] * _dot(gated, wout_ref[...])


def _hgrn_pass(h, hc, mods, g, win, lb_logits, layer, reverse, gn=None, wout=None, o_fw=None):
    b, s, d = h.shape
    tm = TM_SCAN
    assert hc.shape[1] == tm
    n_lat = s // tm
    depth = lb_logits.shape[0]
    if reverse:
        lat = lambda i, j: (i, jnp.where(j == 0, n_lat - 1, n_lat - j), 0)
    else:
        lat = lambda i, j: (i, jnp.maximum(j - 1, 0), 0)
    tile = pl.BlockSpec((None, tm, d), lat)
    wcol = lambda k: _const_spec((d, d), lambda i, j: (0, k))
    in_specs = [
        tile,
        pl.BlockSpec((None, tm, d), lambda i, j: (i, 0, 0)),
        pl.BlockSpec((None, N_MOD, d), lambda i, j: (jnp.where(j == 0, b, i), 0, 0)),
        _const_spec((1, d), lambda i, j: (0, 0)),
        wcol(0), wcol(1), wcol(3 if reverse else 2),
    ]
    args = [h, hc, mods, g.reshape(1, d), win, win, win]
    if reverse:
        in_specs += [wcol(4), _const_spec((d, d), lambda i, j: (0, 0))]
        args += [win, wout]
    in_specs.append(_const_spec((depth, 2, d), lambda i, j: (0, 0, 0)))
    args.append(lb_logits)
    if reverse:
        in_specs += [_const_spec((1, d), lambda i, j: (0, 0)), tile]
        args += [gn.reshape(1, d), o_fw]
    return pl.pallas_call(
        functools.partial(_hgrn_kernel, layer=layer, reverse=reverse),
        grid=(b, n_lat + 1),
        in_specs=in_specs,
        out_specs=tile,
        out_shape=jax.ShapeDtypeStruct(h.shape, _F32),
        scratch_shapes=[
            pltpu.VMEM((HG_HEADS, d // HG_HEADS, d // HG_HEADS), _F32),
            pltpu.VMEM((tm, d), _BF16), pltpu.VMEM((tm, d), _BF16),
            pltpu.VMEM((tm, d), _BF16), pltpu.VMEM((tm, d), _BF16),
            pltpu.VMEM((tm, d), _F32),
        ],
        compiler_params=_params(),
        name="hgrn_bwd" if reverse else "hgrn_fwd",
    )(*args)


def _split_gu(w_gu, w_dn):
    d, f2 = w_gu.shape
    f = f2 // 2
    n = f // FFN_CHUNK
    gu = w_gu.astype(_BF16).reshape(d, 2, n, FFN_CHUNK)
    gu = jnp.transpose(gu, (2, 0, 1, 3)).reshape(n, d, 2 * FFN_CHUNK)
    return gu, w_dn.astype(_BF16).reshape(n, FFN_CHUNK, d)


def kernel(x, c, ctx, c_ctx, ada_w, ada_b, norm_g, ffn_w_gu, ffn_w_down, conv_w_in, conv_w, conv_w_out,
           hg_w_in, hg_lb_logits, hg_gnorm_g, hg_w_out, final_norm_g):
    depth = ada_w.shape[0]
    b, _, d = x.shape
    assert depth == 2 and b < COND_ROWS
    cc = jnp.zeros((COND_ROWS, d), _F32).at[:b].set(c).at[b].set(c_ctx)
    mods = _ada_mods(cc, ada_w, ada_b)
    lat_row = lambda i: i
    ctx_row = lambda i: b

    h, hc = x, ctx
    m0 = mods[0]
    gu0, dn0 = _split_gu(ffn_w_gu[0, 0], ffn_w_down[0, 0])
    gu1, dn1 = _split_gu(ffn_w_gu[0, 1], ffn_w_down[0, 1])
    win = conv_w_in[0].astype(_BF16)
    wout = conv_w_out[0].astype(_BF16)
    h = _ffn_half(h, m0, lat_row, norm_g[0, 0], gu0, dn0, 0, TM_FFN)
    hc = _ffn_half(hc, m0, ctx_row, norm_g[0, 0], gu0, dn0, 0, TM_FFN)
    h = _conv_mixer(h, m0, lat_row, norm_g[0, 1], win, conv_w[0], wout, TM_CONV, True)
    hc = _conv_mixer(hc, m0, ctx_row, norm_g[0, 1], win, conv_w[0], wout, hc.shape[1], False)
    h = _ffn_half(h, m0, lat_row, norm_g[0, 2], gu1, dn1, 6, TM_FFN)
    hc = _ffn_half(hc, m0, ctx_row, norm_g[0, 2], gu1, dn1, 6, TM_FFN)

    m1 = mods[1]
    gu0, dn0 = _split_gu(ffn_w_gu[1, 0], ffn_w_down[1, 0])
    gu1, dn1 = _split_gu(ffn_w_gu[1, 1], ffn_w_down[1, 1])
    hwin = hg_w_in[0].astype(_BF16)
    h = _ffn_half(h, m1, lat_row, norm_g[1, 0], gu0, dn0, 0, TM_FFN)
    hc = _ffn_half(hc, m1, ctx_row, norm_g[1, 0], gu0, dn0, 0, TM_FFN)
    o_fw = _hgrn_pass(h, hc, m1, norm_g[1, 1], hwin, hg_lb_logits, 1, False)
    h = _hgrn_pass(h, hc, m1, norm_g[1, 1], hwin, hg_lb_logits, 1, True,
                   gn=hg_gnorm_g[0], wout=hg_w_out[0].astype(_BF16), o_fw=o_fw)
    return _ffn_half(h, m1, lat_row, norm_g[1, 2], gu1, dn1, 6, TM_FFN, final_g=final_norm_g)
```

```python
import functools

import jax
import jax.numpy as jnp
from jax import lax
from jax.experimental import pallas as pl
from jax.experimental.pallas import tpu as pltpu

EPS = 1e-6
GRID_W = 64
N_MOD = 9
HG_HEADS = 8
HG_CHUNK = 64
COND_ROWS = 16

FFN_CHUNK = 256
TM_FFN = 512
TM_SUB = 256
TM_CONV = 256
TM_SCAN = 256
VMEM_LIMIT = 56 * 1024 * 1024

_F32 = jnp.float32
_BF16 = jnp.bfloat16


def _params():
    return pltpu.CompilerParams(dimension_semantics=("arbitrary", "arbitrary"),
                                vmem_limit_bytes=VMEM_LIMIT)


def _const_spec(shape, index_map):
    return pl.BlockSpec(shape, index_map, pipeline_mode=pl.Buffered(1))


def _rms_norm(x, g):
    return x * lax.rsqrt(jnp.mean(x * x, axis=-1, keepdims=True) + EPS) * g


def _modulated(x, g, m, idx):
    return _rms_norm(x, g) * (1.0 + m[idx + 1:idx + 2, :]) + m[idx:idx + 1, :]


def _silu(x):
    return x * jax.nn.sigmoid(x)


def _dot(a, b):
    return jnp.dot(a, b, preferred_element_type=_F32)


def _ada_kernel(c_ref, w_ref, b_ref, o_ref):
    s = _silu(c_ref[...]).astype(_BF16)
    o_ref[...] = _dot(s, w_ref[...].astype(_BF16)) + b_ref[...]


def _ada_mods(cc, ada_w, ada_b):
    depth, d, nd = ada_w.shape
    out = pl.pallas_call(
        _ada_kernel,
        grid=(depth, nd // d),
        in_specs=[
            pl.BlockSpec((COND_ROWS, d), lambda l, k: (0, 0)),
            pl.BlockSpec((None, d, d), lambda l, k: (l, 0, k)),
            pl.BlockSpec((None, 1, d), lambda l, k: (l, 0, k)),
        ],
        out_specs=pl.BlockSpec((None, COND_ROWS, d), lambda l, k: (l, 0, k)),
        out_shape=jax.ShapeDtypeStruct((depth, COND_ROWS, nd), _F32),
        compiler_params=_params(),
        name="ada_mods",
    )(cc, ada_w, ada_b.reshape(depth, 1, nd))
    return out.reshape(depth, COND_ROWS, nd // d, d)


def _ffn_kernel(*refs, idx, n_chunk, final):
    if final:
        h_ref, m_ref, g_ref, wgu_ref, wdn_ref, fg_ref, o_ref = refs
    else:
        h_ref, m_ref, g_ref, wgu_ref, wdn_ref, o_ref = refs
    m = m_ref[...]
    g = g_ref[...]
    fc = wdn_ref.shape[1]
    n_sub = h_ref.shape[0] // TM_SUB
    items = [(r, c) for r in range(n_sub) for c in range(n_chunk)]
    xs, acts, accs = {}, {}, {}

    def gate_up(i):
        r, c = items[i]
        if c == 0:
            xs[r] = h_ref[r * TM_SUB:(r + 1) * TM_SUB, :]
            acts[r] = _modulated(xs[r], g, m, idx).astype(_BF16)
            accs[r] = jnp.zeros(xs[r].shape, _F32)
        return _dot(acts[r], wgu_ref[c])

    gu_next = gate_up(0)
    for i, (r, c) in enumerate(items):
        gu = gu_next
        if i + 1 < len(items):
            gu_next = gate_up(i + 1)
        act = (_silu(gu[:, :fc]) * gu[:, fc:]).astype(_BF16)
        accs[r] = accs[r] + _dot(act, wdn_ref[c])
        if c == n_chunk - 1:
            out = xs[r] + 0.5 * m[idx + 2:idx + 3, :] * accs[r]
            if final:
                out = _rms_norm(out, fg_ref[...])
            o_ref[r * TM_SUB:(r + 1) * TM_SUB, :] = out


def _ffn_half(h, mods, row_of_batch, g, wgu, wdn, idx, tm, final_g=None):
    b, s, d = h.shape
    n_chunk, _, fc2 = wgu.shape
    tm = min(tm, s)
    in_specs = [
        pl.BlockSpec((None, tm, d), lambda i, j: (i, j, 0)),
        pl.BlockSpec((None, N_MOD, d), lambda i, j: (row_of_batch(i), 0, 0)),
        _const_spec((1, d), lambda i, j: (0, 0)),
        _const_spec((n_chunk, d, fc2), lambda i, j: (0, 0, 0)),
        _const_spec((n_chunk, fc2 // 2, d), lambda i, j: (0, 0, 0)),
    ]
    args = [h, mods, g.reshape(1, d), wgu, wdn]
    if final_g is not None:
        in_specs.append(_const_spec((1, d), lambda i, j: (0, 0)))
        args.append(final_g.reshape(1, d))
    return pl.pallas_call(
        functools.partial(_ffn_kernel, idx=idx, n_chunk=n_chunk, final=final_g is not None),
        grid=(b, s // tm),
        in_specs=in_specs,
        out_specs=pl.BlockSpec((None, tm, d), lambda i, j: (i, j, 0)),
        out_shape=jax.ShapeDtypeStruct(h.shape, _F32),
        compiler_params=_params(),
        name="ffn_half",
    )(*args)


def _conv_kernel(*refs, period, grid_rows):
    if grid_rows:
        h_ref, hp_ref, hn_ref, m_ref, g_ref, win_ref, cw_ref, wout_ref, o_ref = refs
    else:
        h_ref, m_ref, g_ref, win_ref, cw_ref, wout_ref, o_ref = refs
    x = h_ref[...]
    tm, d = x.shape
    half = d // 2
    m = m_ref[...]
    g = g_ref[...]
    cw = cw_ref[...]
    a = _modulated(x, g, m, 3).astype(_BF16)
    bg = _dot(a, win_ref[:, :d])
    z = _dot(a, win_ref[:, d:2 * d]) * _dot(a, win_ref[:, 2 * d:])

    def seq_taps(zz, w):
        pos = lax.broadcasted_iota(jnp.int32, zz.shape, 0) % period
        left = jnp.where(pos == 0, 0.0, pltpu.roll(zz, 1, 0))
        right = jnp.where(pos == period - 1, 0.0, pltpu.roll(zz, zz.shape[0] - 1, 0))
        return left * w[0:1] + zz * w[1:2] + right * w[2:3]

    if grid_rows:
        j = pl.program_id(1)
        last = pl.num_programs(1) - 1

        def halo(ref, keep):
            ah = _modulated(ref[...], g, m, 3).astype(_BF16)
            zh = _dot(ah, win_ref[:, d + half:2 * d]) * _dot(ah, win_ref[:, 2 * d + half:])
            return jnp.where(keep, zh, 0.0)

        zv = z[:, half:]
        up = jnp.concatenate([halo(hp_ref, j > 0), zv[:tm - GRID_W]], axis=0)
        dn = jnp.concatenate([zv[GRID_W:], halo(hn_ref, j < last)], axis=0)
        wv = cw[:, half:]
        zc = jnp.concatenate(
            [seq_taps(z[:, :half], cw[:, :half]), up * wv[0:1] + zv * wv[1:2] + dn * wv[2:3]], axis=1)
    else:
        zc = seq_taps(z, cw)
    y = _dot((bg * zc).astype(_BF16), wout_ref[...])
    o_ref[...] = x + m[5:6, :] * y


def _conv_mixer(h, mods, row_of_batch, g, win, cw, wout, tm, grid_rows):
    b, s, d = h.shape
    spec_main = pl.BlockSpec((None, tm, d), lambda i, j: (i, j, 0))
    in_specs = [spec_main]
    args = [h]
    if grid_rows:
        r = tm // GRID_W
        n_rows = s // GRID_W
        in_specs += [
            pl.BlockSpec((None, GRID_W, d), lambda i, j: (i, jnp.maximum(j * r - 1, 0), 0)),
            pl.BlockSpec((None, GRID_W, d), lambda i, j: (i, jnp.minimum((j + 1) * r, n_rows - 1), 0)),
        ]
        args += [h, h]
    in_specs += [
        pl.BlockSpec((None, N_MOD, d), lambda i, j: (row_of_batch(i), 0, 0)),
        _const_spec((1, d), lambda i, j: (0, 0)),
        _const_spec((d, 3 * d), lambda i, j: (0, 0)),
        _const_spec((3, d), lambda i, j: (0, 0)),
        _const_spec((d, d), lambda i, j: (0, 0)),
    ]
    args += [mods, g.reshape(1, d), win, cw, wout]
    return pl.pallas_call(
        functools.partial(_conv_kernel, period=GRID_W if grid_rows else s, grid_rows=grid_rows),
        grid=(b, s // tm),
        in_specs=in_specs,
        out_specs=spec_main,
        out_shape=jax.ShapeDtypeStruct(h.shape, _F32),
        compiler_params=_params(),
        name="conv_mixer",
    )(*args)


def _chunk_cumsum(x, reverse):
    n = x.shape[0]
    pos = lax.broadcasted_iota(jnp.int32, x.shape, 0) % HG_CHUNK
    step = 1
    while step < HG_CHUNK:
        if reverse:
            x = x + jnp.where(pos < HG_CHUNK - step, pltpu.roll(x, n - step, 0), 0.0)
        else:
            x = x + jnp.where(pos >= step, pltpu.roll(x, step, 0), 0.0)
        step *= 2
    return x


def _lower_bound(lb_ref, layer, direction):
    depth = lb_ref.shape[0]
    rows = [lb_ref[l][direction:direction + 1, :] for l in range(depth)]
    mx = functools.reduce(jnp.maximum, rows)
    ex = [jnp.exp(r - mx) for r in rows]
    den = functools.reduce(lambda u, v: u + v, ex)
    p = [e / den for e in ex]
    return functools.reduce(lambda u, v: u + v, p[:layer + 1]) - p[0]


def _scan_tile(q, v, f_logit, lb, s_ref, qt_ref, kt_ref, kd_ref, v_ref, o_ref, reverse):
    tm, d = q.shape
    dk = d // HG_HEADS
    n_chunk = tm // HG_CHUNK
    f = lb + (1.0 - lb) * jax.nn.sigmoid(f_logit)
    k = 1.0 - f
    gc = _chunk_cumsum(jnp.log(f), reverse)
    qt_ref[...] = (q * jnp.exp(gc)).astype(_BF16)
    kt_ref[...] = (k * jnp.exp(-gc)).astype(_BF16)
    v_ref[...] = v.astype(_BF16)
    edge = 0 if reverse else HG_CHUNK - 1
    g_last = [gc[c * HG_CHUNK + edge:c * HG_CHUNK + edge + 1, :] for c in range(n_chunk)]
    for c in range(n_chunk):
        rows = slice(c * HG_CHUNK, (c + 1) * HG_CHUNK)
        kd_ref[rows, :] = (k[rows] * jnp.exp(g_last[c] - gc[rows])).astype(_BF16)
    ti = lax.broadcasted_iota(jnp.int32, (HG_CHUNK, HG_CHUNK), 0)
    si = lax.broadcasted_iota(jnp.int32, (HG_CHUNK, HG_CHUNK), 1)
    mask = (ti <= si) if reverse else (ti >= si)
    nt = (((1,), (1,)), ((), ()))
    tn = (((0,), (0,)), ((), ()))
    for ci in range(n_chunk):
        c = n_chunk - 1 - ci if reverse else ci
        rows = slice(c * HG_CHUNK, (c + 1) * HG_CHUNK)
        decay = jnp.exp(g_last[c])
        for h in range(HG_HEADS):
            cols = slice(h * dk, (h + 1) * dk)
            qt = qt_ref[rows, cols]
            vv = v_ref[rows, cols]
            st = s_ref[h]
            att = lax.dot_general(qt, kt_ref[rows, cols], nt, preferred_element_type=_F32)
            att = jnp.where(mask, att, 0.0).astype(_BF16)
            o_ref[rows, cols] = _dot(att, vv) + lax.dot_general(
                qt, st.astype(_BF16), nt, preferred_element_type=_F32)
            s_ref[h] = st * decay[:, cols] + lax.dot_general(
                vv, kd_ref[rows, cols], tn, preferred_element_type=_F32)


def _hgrn_kernel(*refs, layer, reverse):
    if reverse:
        (h_ref, hc_ref, m_ref, g_ref, wq_ref, wi_ref, wf_ref, wog_ref, wout_ref, lb_ref, gn_ref, ofw_ref,
         out_ref, s_ref, qt_ref, kt_ref, kd_ref, v_ref, o_ref) = refs
    else:
        (h_ref, hc_ref, m_ref, g_ref, wq_ref, wi_ref, wf_ref, lb_ref,
         out_ref, s_ref, qt_ref, kt_ref, kd_ref, v_ref, o_ref) = refs
    j = pl.program_id(1)

    @pl.when(j == 0)
    def _():
        s_ref[...] = jnp.zeros(s_ref.shape, _F32)

    x = jnp.where(j == 0, hc_ref[...], h_ref[...])
    m = m_ref[...]
    a = _modulated(x, g_ref[...], m, 3).astype(_BF16)
    q = _silu(_dot(a, wq_ref[...]))
    v = _dot(a, wi_ref[...])
    lb = _lower_bound(lb_ref, layer, 1 if reverse else 0)
    _scan_tile(q, v, _dot(a, wf_ref[...]), lb, s_ref, qt_ref, kt_ref, kd_ref, v_ref, o_ref, reverse)
    if not reverse:
        out_ref[...] = o_ref[...]
        return

    @pl.when(j > 0)
    def _():
        d = x.shape[1]
        dk = d // HG_HEADS
        o = ofw_ref[...] + o_ref[...]
        gn = gn_ref[...]
        normed = [_rms_norm(o[:, h * dk:(h + 1) * dk], gn[:, h * dk:(h + 1) * dk]) for h in range(HG_HEADS)]
        gated = (jnp.concatenate(normed, axis=1) * _silu(_dot(a, wog_ref[...]))).astype(_BF16)
        out_ref[...] = x + m[5:6, :] * _dot(gated, wout_ref[...])


def _hgrn_pass(h, hc, mods, g, win, lb_logits, layer, reverse, gn=None, wout=None, o_fw=None):
    b, s, d = h.shape
    tm = TM_SCAN
    assert hc.shape[1] == tm
    n_lat = s // tm
    depth = lb_logits.shape[0]
    if reverse:
        lat = lambda i, j: (i, jnp.where(j == 0, n_lat - 1, n_lat - j), 0)
    else:
        lat = lambda i, j: (i, jnp.maximum(j - 1, 0), 0)
    tile = pl.BlockSpec((None, tm, d), lat)
    wcol = lambda k: _const_spec((d, d), lambda i, j: (0, k))
    in_specs = [
        tile,
        pl.BlockSpec((None, tm, d), lambda i, j: (i, 0, 0)),
        pl.BlockSpec((None, N_MOD, d), lambda i, j: (jnp.where(j == 0, b, i), 0, 0)),
        _const_spec((1, d), lambda i, j: (0, 0)),
        wcol(0), wcol(1), wcol(3 if reverse else 2),
    ]
    args = [h, hc, mods, g.reshape(1, d), win, win, win]
    if reverse:
        in_specs += [wcol(4), _const_spec((d, d), lambda i, j: (0, 0))]
        args += [win, wout]
    in_specs.append(_const_spec((depth, 2, d), lambda i, j: (0, 0, 0)))
    args.append(lb_logits)
    if reverse:
        in_specs += [_const_spec((1, d), lambda i, j: (0, 0)), tile]
        args += [gn.reshape(1, d), o_fw]
    return pl.pallas_call(
        functools.partial(_hgrn_kernel, layer=layer, reverse=reverse),
        grid=(b, n_lat + 1),
        in_specs=in_specs,
        out_specs=tile,
        out_shape=jax.ShapeDtypeStruct(h.shape, _F32),
        scratch_shapes=[
            pltpu.VMEM((HG_HEADS, d // HG_HEADS, d // HG_HEADS), _F32),
            pltpu.VMEM((tm, d), _BF16), pltpu.VMEM((tm, d), _BF16),
            pltpu.VMEM((tm, d), _BF16), pltpu.VMEM((tm, d), _BF16),
            pltpu.VMEM((tm, d), _F32),
        ],
        compiler_params=_params(),
        name="hgrn_bwd" if reverse else "hgrn_fwd",
    )(*args)


def _split_gu(w_gu, w_dn):
    d, f2 = w_gu.shape
    f = f2 // 2
    n = f // FFN_CHUNK
    gu = w_gu.astype(_BF16).reshape(d, 2, n, FFN_CHUNK)
    gu = jnp.transpose(gu, (2, 0, 1, 3)).reshape(n, d, 2 * FFN_CHUNK)
    return gu, w_dn.astype(_BF16).reshape(n, FFN_CHUNK, d)


def kernel(x, c, ctx, c_ctx, ada_w, ada_b, norm_g, ffn_w_gu, ffn_w_down, conv_w_in, conv_w, conv_w_out,
           hg_w_in, hg_lb_logits, hg_gnorm_g, hg_w_out, final_norm_g):
    depth = ada_w.shape[0]
    b, _, d = x.shape
    assert depth == 2 and b < COND_ROWS
    cc = jnp.zeros((COND_ROWS, d), _F32).at[:b].set(c).at[b].set(c_ctx)
    mods = _ada_mods(cc, ada_w, ada_b)
    lat_row = lambda i: i
    ctx_row = lambda i: b

    h, hc = x, ctx
    m0 = mods[0]
    gu0, dn0 = _split_gu(ffn_w_gu[0, 0], ffn_w_down[0, 0])
    gu1, dn1 = _split_gu(ffn_w_gu[0, 1], ffn_w_down[0, 1])
    win = conv_w_in[0].astype(_BF16)
    wout = conv_w_out[0].astype(_BF16)
    h = _ffn_half(h, m0, lat_row, norm_g[0, 0], gu0, dn0, 0, TM_FFN)
    hc = _ffn_half(hc, m0, ctx_row, norm_g[0, 0], gu0, dn0, 0, TM_FFN)
    h = _conv_mixer(h, m0, lat_row, norm_g[0, 1], win, conv_w[0], wout, TM_CONV, True)
    hc = _conv_mixer(hc, m0, ctx_row, norm_g[0, 1], win, conv_w[0], wout, hc.shape[1], False)
    h = _ffn_half(h, m0, lat_row, norm_g[0, 2], gu1, dn1, 6, TM_FFN)
    hc = _ffn_half(hc, m0, ctx_row, norm_g[0, 2], gu1, dn1, 6, TM_FFN)

    m1 = mods[1]
    gu0, dn0 = _split_gu(ffn_w_gu[1, 0], ffn_w_down[1, 0])
    gu1, dn1 = _split_gu(ffn_w_gu[1, 1], ffn_w_down[1, 1])
    hwin = hg_w_in[0].astype(_BF16)
    h = _ffn_half(h, m1, lat_row, norm_g[1, 0], gu0, dn0, 0, TM_FFN)
    hc = _ffn_half(hc, m1, ctx_row, norm_g[1, 0], gu0, dn0, 0, TM_FFN)
    o_fw = _hgrn_pass(h, hc, m1, norm_g[1, 1], hwin, hg_lb_logits, 1, False)
    h = _hgrn_pass(h, hc, m1, norm_g[1, 1], hwin, hg_lb_logits, 1, True,
                   gn=hg_gnorm_g[0], wout=hg_w_out[0].astype(_BF16), o_fw=o_fw)
    return _ffn_half(h, m1, lat_row, norm_g[1, 2], gu1, dn1, 6, TM_FFN, final_g=final_norm_g)
```

```python
import functools

import jax
import jax.numpy as jnp
from jax import lax
from jax.experimental import pallas as pl
from jax.experimental.pallas import tpu as pltpu

EPS = 1e-6
GRID_W = 64
N_MOD = 9
HG_HEADS = 8
HG_CHUNK = 64
COND_ROWS = 16
SUBLANES = 8

FFN_CHUNK = 256
TM_FFN = 512
TM_PROJ = 512
TM_SUB = 256
TM_CONV = 512
TM_SCAN = 256
VMEM_LIMIT = 56 * 1024 * 1024

_F32 = jnp.float32
_BF16 = jnp.bfloat16
_NT = (((1,), (1,)), ((), ()))
_TN = (((0,), (0,)), ((), ()))


def _params():
    return pltpu.CompilerParams(dimension_semantics=("arbitrary", "arbitrary"),
                                vmem_limit_bytes=VMEM_LIMIT)


def _const_spec(shape, index_map):
    return pl.BlockSpec(shape, index_map, pipeline_mode=pl.Buffered(1))


def _rms_norm(x, g):
    return x * lax.rsqrt(jnp.mean(x * x, axis=-1, keepdims=True) + EPS) * g


def _modulated(x, g, m, idx):
    return _rms_norm(x, g) * (1.0 + m[idx + 1:idx + 2, :]) + m[idx:idx + 1, :]


def _silu(x):
    return x * jax.nn.sigmoid(x)


def _dot(a, b):
    return jnp.dot(a, b, preferred_element_type=_F32)


def _ada_kernel(c_ref, w_ref, b_ref, o_ref):
    s = _silu(c_ref[...]).astype(_BF16)
    o_ref[...] = _dot(s, w_ref[...].astype(_BF16)) + b_ref[...]


def _ada_mods(cc, ada_w, ada_b):
    depth, d, nd = ada_w.shape
    out = pl.pallas_call(
        _ada_kernel,
        grid=(depth, nd // d),
        in_specs=[
            pl.BlockSpec((COND_ROWS, d), lambda l, k: (0, 0)),
            pl.BlockSpec((None, d, d), lambda l, k: (l, 0, k)),
            pl.BlockSpec((None, 1, d), lambda l, k: (l, 0, k)),
        ],
        out_specs=pl.BlockSpec((None, COND_ROWS, d), lambda l, k: (l, 0, k)),
        out_shape=jax.ShapeDtypeStruct((depth, COND_ROWS, nd), _F32),
        compiler_params=_params(),
        name="ada_mods",
    )(cc, ada_w, ada_b.reshape(depth, 1, nd))
    return out.reshape(depth, COND_ROWS, nd // d, d)


def _ffn_kernel(*refs, idx, n_chunk, final):
    if final:
        h_ref, m_ref, g_ref, wgu_ref, wdn_ref, fg_ref, o_ref = refs
    else:
        h_ref, m_ref, g_ref, wgu_ref, wdn_ref, o_ref = refs
    m = m_ref[...]
    g = g_ref[...]
    fc = wdn_ref.shape[1]
    n_sub = h_ref.shape[0] // TM_SUB
    items = [(r, c) for r in range(n_sub) for c in range(n_chunk)]
    xs, acts, accs = {}, {}, {}

    def gate_up(i):
        r, c = items[i]
        if c == 0:
            xs[r] = h_ref[r * TM_SUB:(r + 1) * TM_SUB, :]
            acts[r] = _modulated(xs[r], g, m, idx).astype(_BF16)
            accs[r] = jnp.zeros(xs[r].shape, _F32)
        return _dot(acts[r], wgu_ref[c])

    gu_next = gate_up(0)
    for i, (r, c) in enumerate(items):
        gu = gu_next
        if i + 1 < len(items):
            gu_next = gate_up(i + 1)
        act = (_silu(gu[:, :fc]) * gu[:, fc:]).astype(_BF16)
        accs[r] = accs[r] + _dot(act, wdn_ref[c])
        if c == n_chunk - 1:
            out = xs[r] + 0.5 * m[idx + 2:idx + 3, :] * accs[r]
            if final:
                out = _rms_norm(out, fg_ref[...])
            o_ref[r * TM_SUB:(r + 1) * TM_SUB, :] = out


def _ffn_half(h, mods, row_of_batch, g, wgu, wdn, idx, tm, final_g=None):
    b, s, d = h.shape
    n_chunk, _, fc2 = wgu.shape
    tm = min(tm, s)
    in_specs = [
        pl.BlockSpec((None, tm, d), lambda i, j: (i, j, 0)),
        pl.BlockSpec((None, N_MOD, d), lambda i, j: (row_of_batch(i), 0, 0)),
        _const_spec((1, d), lambda i, j: (0, 0)),
        _const_spec((n_chunk, d, fc2), lambda i, j: (0, 0, 0)),
        _const_spec((n_chunk, fc2 // 2, d), lambda i, j: (0, 0, 0)),
    ]
    args = [h, mods, g.reshape(1, d), wgu, wdn]
    if final_g is not None:
        in_specs.append(_const_spec((1, d), lambda i, j: (0, 0)))
        args.append(final_g.reshape(1, d))
    return pl.pallas_call(
        functools.partial(_ffn_kernel, idx=idx, n_chunk=n_chunk, final=final_g is not None),
        grid=(b, s // tm),
        in_specs=in_specs,
        out_specs=pl.BlockSpec((None, tm, d), lambda i, j: (i, j, 0)),
        out_shape=jax.ShapeDtypeStruct(h.shape, _F32),
        compiler_params=_params(),
        name="ffn_half",
    )(*args)


def _conv_kernel(*refs, period, grid_rows):
    if grid_rows:
        h_ref, hp_ref, hn_ref, m_ref, g_ref, win_ref, cw_ref, wout_ref, o_ref = refs
    else:
        h_ref, m_ref, g_ref, win_ref, cw_ref, wout_ref, o_ref = refs
    x = h_ref[...]
    tm, d = x.shape
    half = d // 2
    m = m_ref[...]
    g = g_ref[...]
    cw = cw_ref[...]
    a = _modulated(x, g, m, 3).astype(_BF16)

    def seq_taps(zz, w):
        pos = lax.broadcasted_iota(jnp.int32, zz.shape, 0) % period
        left = jnp.where(pos == 0, 0.0, pltpu.roll(zz, 1, 0))
        right = jnp.where(pos == period - 1, 0.0, pltpu.roll(zz, zz.shape[0] - 1, 0))
        return left * w[0:1] + zz * w[1:2] + right * w[2:3]

    if grid_rows:
        j = pl.program_id(1)
        last = pl.num_programs(1) - 1
        ah = _modulated(jnp.concatenate([hp_ref[...], hn_ref[...]], axis=0), g, m, 3).astype(_BF16)
        zh = _dot(ah, win_ref[:, d + half:2 * d]) * _dot(ah, win_ref[:, 2 * d + half:])
    z = _dot(a, win_ref[:, d:2 * d]) * _dot(a, win_ref[:, 2 * d:])
    bg = _dot(a, win_ref[:, :d])
    if grid_rows:
        zv = z[:, half:]
        ext = jnp.concatenate([jnp.where(j > 0, zh[:GRID_W], 0.0), zv,
                               jnp.where(j < last, zh[GRID_W:], 0.0)], axis=0)
        wv = cw[:, half:]
        rows_v = ext[:tm] * wv[0:1] + zv * wv[1:2] + ext[2 * GRID_W:] * wv[2:3]
        zc = jnp.concatenate([seq_taps(z[:, :half], cw[:, :half]), rows_v], axis=1)
    else:
        zc = seq_taps(z, cw)
    y = _dot((bg * zc).astype(_BF16), wout_ref[...])
    o_ref[...] = x + m[5:6, :] * y


def _conv_mixer(h, mods, row_of_batch, g, win, cw, wout, tm, grid_rows):
    b, s, d = h.shape
    tm = min(tm, s)
    spec_main = pl.BlockSpec((None, tm, d), lambda i, j: (i, j, 0))
    in_specs = [spec_main]
    args = [h]
    if grid_rows:
        r = tm // GRID_W
        n_rows = s // GRID_W
        in_specs += [
            pl.BlockSpec((None, GRID_W, d), lambda i, j: (i, jnp.maximum(j * r - 1, 0), 0)),
            pl.BlockSpec((None, GRID_W, d), lambda i, j: (i, jnp.minimum((j + 1) * r, n_rows - 1), 0)),
        ]
        args += [h, h]
    in_specs += [
        pl.BlockSpec((None, N_MOD, d), lambda i, j: (row_of_batch(i), 0, 0)),
        _const_spec((1, d), lambda i, j: (0, 0)),
        _const_spec((d, 3 * d), lambda i, j: (0, 0)),
        _const_spec((3, d), lambda i, j: (0, 0)),
        _const_spec((d, d), lambda i, j: (0, 0)),
    ]
    args += [mods, g.reshape(1, d), win, cw, wout]
    return pl.pallas_call(
        functools.partial(_conv_kernel, period=GRID_W if grid_rows else s, grid_rows=grid_rows),
        grid=(b, s // tm),
        in_specs=in_specs,
        out_specs=spec_main,
        out_shape=jax.ShapeDtypeStruct(h.shape, _F32),
        compiler_params=_params(),
        name="conv_mixer",
    )(*args)


def _lower_bound(lb_ref, layer, direction):
    depth = lb_ref.shape[0]
    rows = [lb_ref[l][direction:direction + 1, :] for l in range(depth)]
    mx = functools.reduce(jnp.maximum, rows)
    ex = [jnp.exp(r - mx) for r in rows]
    den = functools.reduce(lambda u, v: u + v, ex)
    p = [e / den for e in ex]
    return functools.reduce(lambda u, v: u + v, p[:layer + 1]) - p[0]


def _hgrn_proj_kernel(h_ref, m_ref, g_ref, wq_ref, wi_ref, wff_ref, wfb_ref, wog_ref, lb_ref,
                      q_ref, v_ref, lff_ref, lfb_ref, sg_ref, *, layer):
    m = m_ref[...]
    g = g_ref[...]
    gates = ((wff_ref, lff_ref, _lower_bound(lb_ref, layer, 0)),
             (wfb_ref, lfb_ref, _lower_bound(lb_ref, layer, 1)))
    for r in range(h_ref.shape[0] // TM_SUB):
        rows = slice(r * TM_SUB, (r + 1) * TM_SUB)
        a = _modulated(h_ref[rows, :], g, m, 3).astype(_BF16)
        q_ref[rows, :] = _silu(_dot(a, wq_ref[...]))
        v_ref[rows, :] = _dot(a, wi_ref[...]).astype(_BF16)
        for w_ref, out_ref, lb in gates:
            out_ref[rows, :] = jnp.log(lb + (1.0 - lb) * jax.nn.sigmoid(_dot(a, w_ref[...])))
        sg_ref[rows, :] = _silu(_dot(a, wog_ref[...]))


def _hgrn_project(h, mods, row_of_batch, g, win, lb_logits, layer, tm):
    b, s, d = h.shape
    tm = min(tm, s)
    depth = lb_logits.shape[0]
    tile = pl.BlockSpec((None, tm, d), lambda i, j: (i, j, 0))
    wcol = lambda k: _const_spec((d, d), lambda i, j: (0, k))
    f32 = jax.ShapeDtypeStruct(h.shape, _F32)
    return pl.pallas_call(
        functools.partial(_hgrn_proj_kernel, layer=layer),
        grid=(b, s // tm),
        in_specs=[
            tile,
            pl.BlockSpec((None, N_MOD, d), lambda i, j: (row_of_batch(i), 0, 0)),
            _const_spec((1, d), lambda i, j: (0, 0)),
            wcol(0), wcol(1), wcol(2), wcol(3), wcol(4),
            _const_spec((depth, 2, d), lambda i, j: (0, 0, 0)),
        ],
        out_specs=[tile] * 5,
        out_shape=[f32, jax.ShapeDtypeStruct(h.shape, _BF16), f32, f32, f32],
        compiler_params=_params(),
        name="hgrn_proj",
    )(h, mods, g.reshape(1, d), win, win, win, win, win, lb_logits)


def _chunk_cumsum(x, reverse):
    tm, d = x.shape
    groups = HG_CHUNK // SUBLANES
    x3 = x.reshape(tm // SUBLANES, SUBLANES, d)
    sub = lax.broadcasted_iota(jnp.int32, x3.shape, 1)
    step = 1
    while step < SUBLANES:
        if reverse:
            x3 = x3 + jnp.where(sub < SUBLANES - step, pltpu.roll(x3, SUBLANES - step, 1), 0.0)
        else:
            x3 = x3 + jnp.where(sub >= step, pltpu.roll(x3, step, 1), 0.0)
        step *= 2
    edge = 0 if reverse else SUBLANES - 1
    out = [None] * (tm // SUBLANES)
    for c in range(tm // HG_CHUNK):
        order = range(c * groups, (c + 1) * groups)
        carry = None
        for r in (reversed(order) if reverse else order):
            blk = x3[r] if carry is None else x3[r] + carry
            out[r] = blk
            carry = blk[edge:edge + 1, :]
    return jnp.concatenate(out, axis=0)


def _hgrn_gates(q, lf, st, qt_ref, kt_ref, kd_ref, dec_ref, reverse):
    n_chunk = q.shape[0] // HG_CHUNK
    k = 1.0 - jnp.exp(lf)
    gc = _chunk_cumsum(lf, reverse)
    qt_ref[st] = (q * jnp.exp(gc)).astype(_BF16)
    kt_ref[st] = (k * jnp.exp(-gc)).astype(_BF16)
    edge = 0 if reverse else HG_CHUNK - 1
    for c in range(n_chunk):
        rows = slice(c * HG_CHUNK, (c + 1) * HG_CHUNK)
        g_last = gc[c * HG_CHUNK + edge:c * HG_CHUNK + edge + 1, :]
        kd_ref[st, rows, :] = (k[rows] * jnp.exp(g_last - gc[rows])).astype(_BF16)
        dec_ref[st, c:c + 1, :] = jnp.exp(g_last)


def _hgrn_scan(st, s_ref, qt_ref, kt_ref, kd_ref, dec_ref, v_tile, store, reverse):
    tm, d = qt_ref.shape[1:]
    dk = d // HG_HEADS
    n_chunk = tm // HG_CHUNK
    ti = lax.broadcasted_iota(jnp.int32, (HG_CHUNK, HG_CHUNK), 0)
    si = lax.broadcasted_iota(jnp.int32, (HG_CHUNK, HG_CHUNK), 1)
    mask = (ti <= si) if reverse else (ti >= si)
    tiles = [(c, h) for c in range(n_chunk) for h in range(HG_HEADS)]

    def piece(ref, t):
        c, h = t
        return ref[st, c * HG_CHUNK:(c + 1) * HG_CHUNK, h * dk:(h + 1) * dk]

    def v_piece(t):
        c, h = t
        return v_tile[c * HG_CHUNK:(c + 1) * HG_CHUNK, h * dk:(h + 1) * dk]

    att = {t: lax.dot_general(piece(qt_ref, t), piece(kt_ref, t), _NT, preferred_element_type=_F32)
           for t in tiles}
    upd = {t: lax.dot_general(v_piece(t), piece(kd_ref, t), _TN, preferred_element_type=_F32)
           for t in tiles}
    intra = {t: _dot(jnp.where(mask, att[t], 0.0).astype(_BF16), v_piece(t)) for t in tiles}
    for ci in range(n_chunk):
        c = n_chunk - 1 - ci if reverse else ci
        decay = dec_ref[st, c:c + 1, :]
        for h in range(HG_HEADS):
            state = s_ref[h]
            inter = lax.dot_general(piece(qt_ref, (c, h)), state.astype(_BF16), _NT,
                                    preferred_element_type=_F32)
            store(c, h, intra[(c, h)] + inter)
            s_ref[h] = state * decay[:, h * dk:(h + 1) * dk] + upd[(c, h)]


def _hgrn_scan_kernel(*refs, reverse):
    if reverse:
        (q0_ref, q1_ref, qc_ref, lf0_ref, lf1_ref, lfc_ref, v_ref, vc_ref, sg_ref, ofw_ref, h_ref, m_ref,
         gn_ref, wout_ref, out_ref, s_ref, qt_ref, kt_ref, kd_ref, dec_ref, o_ref) = refs
    else:
        (q0_ref, q1_ref, qc_ref, lf0_ref, lf1_ref, lfc_ref, v_ref, vc_ref,
         out_ref, s_ref, qt_ref, kt_ref, kd_ref, dec_ref) = refs
    step = pl.program_id(1)
    first = step == 0
    tm, d = q0_ref.shape
    dk = d // HG_HEADS

    @pl.when(first)
    def _():
        s_ref[...] = jnp.zeros(s_ref.shape, _F32)
        for ref in (qt_ref, kt_ref, kd_ref, dec_ref):
            ref[1] = jnp.zeros(ref.shape[1:], ref.dtype)

    qs = (jnp.where(first, qc_ref[...], q0_ref[...]), q1_ref[...])
    lfs = (jnp.where(first, lfc_ref[...], lf0_ref[...]), lf1_ref[...])
    out_rows = (tm, 0) if reverse else (0, tm)

    for half in range(2):
        prev = 1 - half
        base = out_rows[half]
        _hgrn_gates(qs[half], lfs[half], half, qt_ref, kt_ref, kd_ref, dec_ref, reverse)
        v_tile = v_ref[base:base + tm, :]
        if half == 1:
            v_tile = jnp.where(first, vc_ref[...], v_tile)

        def store(c, h, val, half=half, base=base):
            rows = slice(c * HG_CHUNK, (c + 1) * HG_CHUNK)
            if reverse:
                o_ref[half, rows, h * dk:(h + 1) * dk] = val
            else:
                out_ref[base + c * HG_CHUNK:base + (c + 1) * HG_CHUNK, h * dk:(h + 1) * dk] = val

        _hgrn_scan(prev, s_ref, qt_ref, kt_ref, kd_ref, dec_ref, v_tile, store, reverse)
        if reverse:
            o = ofw_ref[base:base + tm, :] + o_ref[half]
            gn = gn_ref[...]
            normed = [_rms_norm(o[:, h * dk:(h + 1) * dk], gn[:, h * dk:(h + 1) * dk]) for h in range(HG_HEADS)]
            gated = (jnp.concatenate(normed, axis=1) * sg_ref[base:base + tm, :]).astype(_BF16)
            out_ref[base:base + tm, :] = (h_ref[base:base + tm, :]
                                          + m_ref[...][5:6, :] * _dot(gated, wout_ref[...]))


def _hgrn_scan_pass(lat, ctx, reverse, sg=None, o_fw=None, h=None, mods=None, gn=None, wout=None):
    q, v, lf = lat
    qc, vc, lfc = ctx
    b, s, d = q.shape
    tm = TM_SCAN
    assert qc.shape[1] == tm and (s // tm) % 2 == 0
    n_lat = s // tm
    n_step = (n_lat + 2) // 2

    def lat_tile(p):
        p = jnp.clip(p, 1, n_lat)
        return n_lat - p if reverse else p - 1

    tile = lambda off: pl.BlockSpec((None, tm, d), lambda i, j: (i, lat_tile(2 * j + off), 0))
    ctx_tile = _const_spec((None, tm, d), lambda i, j: (i, 0, 0))
    pair = pl.BlockSpec((None, 2 * tm, d), lambda i, j: (i, lat_tile(2 * jnp.maximum(j, 1) - reverse) // 2, 0))
    in_specs = [tile(0), tile(1), ctx_tile, tile(0), tile(1), ctx_tile, pair, ctx_tile]
    args = [q, q, qc, lf, lf, lfc, v, vc]
    scratch = [
        pltpu.VMEM((HG_HEADS, d // HG_HEADS, d // HG_HEADS), _F32),
        pltpu.VMEM((2, tm, d), _BF16), pltpu.VMEM((2, tm, d), _BF16), pltpu.VMEM((2, tm, d), _BF16),
        pltpu.VMEM((2, SUBLANES, d), _F32),
    ]
    if reverse:
        in_specs += [pair, pair, pair,
                     pl.BlockSpec((None, N_MOD, d), lambda i, j: (i, 0, 0)),
                     _const_spec((1, d), lambda i, j: (0, 0)),
                     _const_spec((d, d), lambda i, j: (0, 0))]
        args += [sg, o_fw, h, mods, gn.reshape(1, d), wout]
        scratch.append(pltpu.VMEM((2, tm, d), _F32))
    return pl.pallas_call(
        functools.partial(_hgrn_scan_kernel, reverse=reverse),
        grid=(b, n_step),
        in_specs=in_specs,
        out_specs=pair,
        out_shape=jax.ShapeDtypeStruct(q.shape, _F32),
        scratch_shapes=scratch,
        compiler_params=_params(),
        name="hgrn_bwd" if reverse else "hgrn_fwd",
    )(*args)


def _split_gu(w_gu, w_dn):
    d, f2 = w_gu.shape
    f = f2 // 2
    n = f // FFN_CHUNK
    gu = w_gu.astype(_BF16).reshape(d, 2, n, FFN_CHUNK)
    gu = jnp.transpose(gu, (2, 0, 1, 3)).reshape(n, d, 2 * FFN_CHUNK)
    return gu, w_dn.astype(_BF16).reshape(n, FFN_CHUNK, d)


def kernel(x, c, ctx, c_ctx, ada_w, ada_b, norm_g, ffn_w_gu, ffn_w_down, conv_w_in, conv_w, conv_w_out,
           hg_w_in, hg_lb_logits, hg_gnorm_g, hg_w_out, final_norm_g):
    depth = ada_w.shape[0]
    b, _, d = x.shape
    assert depth == 2 and b < COND_ROWS
    cc = jnp.zeros((COND_ROWS, d), _F32).at[:b].set(c).at[b].set(c_ctx)
    mods = _ada_mods(cc, ada_w, ada_b)
    lat_row = lambda i: i
    ctx_row = lambda i: b

    h, hc = x, ctx
    m0 = mods[0]
    gu0, dn0 = _split_gu(ffn_w_gu[0, 0], ffn_w_down[0, 0])
    gu1, dn1 = _split_gu(ffn_w_gu[0, 1], ffn_w_down[0, 1])
    win = conv_w_in[0].astype(_BF16)
    wout = conv_w_out[0].astype(_BF16)
    h = _ffn_half(h, m0, lat_row, norm_g[0, 0], gu0, dn0, 0, TM_FFN)
    hc = _ffn_half(hc, m0, ctx_row, norm_g[0, 0], gu0, dn0, 0, TM_FFN)
    h = _conv_mixer(h, m0, lat_row, norm_g[0, 1], win, conv_w[0], wout, TM_CONV, True)
    hc = _conv_mixer(hc, m0, ctx_row, norm_g[0, 1], win, conv_w[0], wout, TM_CONV, False)
    h = _ffn_half(h, m0, lat_row, norm_g[0, 2], gu1, dn1, 6, TM_FFN)
    hc = _ffn_half(hc, m0, ctx_row, norm_g[0, 2], gu1, dn1, 6, TM_FFN)

    m1 = mods[1]
    gu0, dn0 = _split_gu(ffn_w_gu[1, 0], ffn_w_down[1, 0])
    gu1, dn1 = _split_gu(ffn_w_gu[1, 1], ffn_w_down[1, 1])
    hwin = hg_w_in[0].astype(_BF16)
    h = _ffn_half(h, m1, lat_row, norm_g[1, 0], gu0, dn0, 0, TM_FFN)
    hc = _ffn_half(hc, m1, ctx_row, norm_g[1, 0], gu0, dn0, 0, TM_FFN)
    q, v, lff, lfb, sg = _hgrn_project(h, m1, lat_row, norm_g[1, 1], hwin, hg_lb_logits, 1, TM_PROJ)
    qc, vc, lffc, lfbc, _ = _hgrn_project(hc, m1, ctx_row, norm_g[1, 1], hwin, hg_lb_logits, 1, TM_PROJ)
    o_fw = _hgrn_scan_pass((q, v, lff), (qc, vc, lffc), False)
    h = _hgrn_scan_pass((q, v, lfb), (qc, vc, lfbc), True, sg=sg, o_fw=o_fw, h=h, mods=m1,
                        gn=hg_gnorm_g[0], wout=hg_w_out[0].astype(_BF16))
    return _ffn_half(h, m1, lat_row, norm_g[1, 2], gu1, dn1, 6, TM_FFN, final_g=final_norm_g)
```

```python
import functools

import jax
import jax.numpy as jnp
from jax import lax
from jax.experimental import pallas as pl
from jax.experimental.pallas import tpu as pltpu

EPS = 1e-6
GRID_W = 64
N_MOD = 9
HG_HEADS = 8
HG_CHUNK = 64
COND_ROWS = 16
SUBLANES = 8

FFN_CHUNK = 256
TM_FFN = 1024
TM_PROJ = 512
TM_SUB = 256
TM_CONV = 512
TM_SCAN = 512
VMEM_LIMIT = 56 * 1024 * 1024

_F32 = jnp.float32
_BF16 = jnp.bfloat16
_NT = (((1,), (1,)), ((), ()))
_TN = (((0,), (0,)), ((), ()))


def _params(n_axes=2):
    return pltpu.CompilerParams(dimension_semantics=("arbitrary",) * n_axes,
                                vmem_limit_bytes=VMEM_LIMIT)


def _const_spec(shape, index_map):
    return pl.BlockSpec(shape, index_map, pipeline_mode=pl.Buffered(1))


def _rms_norm(x, g):
    return x * lax.rsqrt(jnp.mean(x * x, axis=-1, keepdims=True) + EPS) * g


def _modulated(x, g, m, idx):
    return _rms_norm(x, g) * (1.0 + m[idx + 1:idx + 2, :]) + m[idx:idx + 1, :]


def _silu(x):
    return x * jax.nn.sigmoid(x)


def _dot(a, b):
    return jnp.dot(a, b, preferred_element_type=_F32)


def _ada_kernel(c_ref, w_ref, b_ref, o_ref):
    s = _silu(c_ref[...]).astype(_BF16)
    o_ref[...] = _dot(s, w_ref[...].astype(_BF16)) + b_ref[...]


def _ada_mods(cc, ada_w, ada_b):
    depth, d, nd = ada_w.shape
    out = pl.pallas_call(
        _ada_kernel,
        grid=(depth, nd // d),
        in_specs=[
            pl.BlockSpec((COND_ROWS, d), lambda l, k: (0, 0)),
            pl.BlockSpec((None, d, d), lambda l, k: (l, 0, k)),
            pl.BlockSpec((None, 1, d), lambda l, k: (l, 0, k)),
        ],
        out_specs=pl.BlockSpec((None, COND_ROWS, d), lambda l, k: (l, 0, k)),
        out_shape=jax.ShapeDtypeStruct((depth, COND_ROWS, nd), _F32),
        compiler_params=_params(),
        name="ada_mods",
    )(cc, ada_w, ada_b.reshape(depth, 1, nd))
    return out.reshape(depth, COND_ROWS, nd // d, d)


def _gate_up_kernel(g_ref, u_ref, o_ref):
    fc = g_ref.shape[1]
    o_ref[:, :fc] = g_ref[...].astype(_BF16)
    o_ref[:, fc:] = u_ref[...].astype(_BF16)


def _chunked_gate_up(w_gu):
    depth, two, d, f2 = w_gu.shape
    n = f2 // 2 // FFN_CHUNK
    return pl.pallas_call(
        _gate_up_kernel,
        grid=(depth, two, n),
        in_specs=[
            pl.BlockSpec((None, None, d, FFN_CHUNK), lambda l, s, c: (l, s, 0, c)),
            pl.BlockSpec((None, None, d, FFN_CHUNK), lambda l, s, c: (l, s, 0, n + c)),
        ],
        out_specs=pl.BlockSpec((None, None, None, d, 2 * FFN_CHUNK), lambda l, s, c: (l, s, c, 0, 0)),
        out_shape=jax.ShapeDtypeStruct((depth, two, n, d, 2 * FFN_CHUNK), _BF16),
        compiler_params=_params(3),
        name="chunk_gate_up",
    )(w_gu, w_gu)


def _ffn_kernel(*refs, idx, n_chunk, final):
    if final:
        h_ref, m_ref, g_ref, wgu_ref, wdn_ref, fg_ref, o_ref = refs
    else:
        h_ref, m_ref, g_ref, wgu_ref, wdn_ref, o_ref = refs
    m = m_ref[...]
    g = g_ref[...]
    fc = wdn_ref.shape[1]
    n_sub = h_ref.shape[0] // TM_SUB
    items = [(r, c) for r in range(n_sub) for c in range(n_chunk)]
    xs, acts, accs = {}, {}, {}

    def gate_up(i):
        r, c = items[i]
        if c == 0:
            xs[r] = h_ref[r * TM_SUB:(r + 1) * TM_SUB, :]
            acts[r] = _modulated(xs[r], g, m, idx).astype(_BF16)
            accs[r] = jnp.zeros(xs[r].shape, _F32)
        return _dot(acts[r], wgu_ref[c])

    gu_next = gate_up(0)
    for i, (r, c) in enumerate(items):
        gu = gu_next
        if i + 1 < len(items):
            gu_next = gate_up(i + 1)
        act = (_silu(gu[:, :fc]) * gu[:, fc:]).astype(_BF16)
        accs[r] = accs[r] + _dot(act, wdn_ref[c])
        if c == n_chunk - 1:
            out = xs[r] + 0.5 * m[idx + 2:idx + 3, :] * accs[r]
            if final:
                out = _rms_norm(out, fg_ref[...])
            o_ref[r * TM_SUB:(r + 1) * TM_SUB, :] = out


def _ffn_half(h, mods, row_of_batch, g, wgu, wdn, layer, which, tm, final_g=None):
    b, s, d = h.shape
    n_chunk, fc2 = wgu.shape[2], wgu.shape[4]
    tm = min(tm, s)
    idx = 6 * which
    in_specs = [
        pl.BlockSpec((None, tm, d), lambda i, j: (i, j, 0)),
        pl.BlockSpec((None, N_MOD, d), lambda i, j: (row_of_batch(i), 0, 0)),
        _const_spec((1, d), lambda i, j: (0, 0)),
        _const_spec((None, None, n_chunk, d, fc2), lambda i, j: (layer, which, 0, 0, 0)),
        _const_spec((None, None, n_chunk, fc2 // 2, d), lambda i, j: (layer, which, 0, 0, 0)),
    ]
    args = [h, mods, g.reshape(1, d), wgu, wdn]
    if final_g is not None:
        in_specs.append(_const_spec((1, d), lambda i, j: (0, 0)))
        args.append(final_g.reshape(1, d))
    return pl.pallas_call(
        functools.partial(_ffn_kernel, idx=idx, n_chunk=n_chunk, final=final_g is not None),
        grid=(b, s // tm),
        in_specs=in_specs,
        out_specs=pl.BlockSpec((None, tm, d), lambda i, j: (i, j, 0)),
        out_shape=jax.ShapeDtypeStruct(h.shape, _F32),
        compiler_params=_params(),
        name="ffn_half",
    )(*args)


def _conv_kernel(*refs, period, grid_rows):
    if grid_rows:
        h_ref, hp_ref, hn_ref, m_ref, g_ref, win_ref, cw_ref, wout_ref, o_ref = refs
    else:
        h_ref, m_ref, g_ref, win_ref, cw_ref, wout_ref, o_ref = refs
    x = h_ref[...]
    tm, d = x.shape
    half = d // 2
    m = m_ref[...]
    g = g_ref[...]
    cw = cw_ref[...]
    a = _modulated(x, g, m, 3).astype(_BF16)

    def seq_taps(zz, w):
        pos = lax.broadcasted_iota(jnp.int32, zz.shape, 0) % period
        left = jnp.where(pos == 0, 0.0, pltpu.roll(zz, 1, 0))
        right = jnp.where(pos == period - 1, 0.0, pltpu.roll(zz, zz.shape[0] - 1, 0))
        return left * w[0:1] + zz * w[1:2] + right * w[2:3]

    if grid_rows:
        j = pl.program_id(1)
        last = pl.num_programs(1) - 1
        ah = _modulated(jnp.concatenate([hp_ref[...], hn_ref[...]], axis=0), g, m, 3).astype(_BF16)
        zh = _dot(ah, win_ref[:, d + half:2 * d]) * _dot(ah, win_ref[:, 2 * d + half:])
    z = _dot(a, win_ref[:, d:2 * d]) * _dot(a, win_ref[:, 2 * d:])
    bg = _dot(a, win_ref[:, :d])
    if grid_rows:
        zv = z[:, half:]
        ext = jnp.concatenate([jnp.where(j > 0, zh[:GRID_W], 0.0), zv,
                               jnp.where(j < last, zh[GRID_W:], 0.0)], axis=0)
        wv = cw[:, half:]
        rows_v = ext[:tm] * wv[0:1] + zv * wv[1:2] + ext[2 * GRID_W:] * wv[2:3]
        zc = jnp.concatenate([seq_taps(z[:, :half], cw[:, :half]), rows_v], axis=1)
    else:
        zc = seq_taps(z, cw)
    y = _dot((bg * zc).astype(_BF16), wout_ref[...])
    o_ref[...] = x + m[5:6, :] * y


def _conv_mixer(h, mods, row_of_batch, g, win, cw, wout, tm, grid_rows):
    b, s, d = h.shape
    tm = min(tm, s)
    spec_main = pl.BlockSpec((None, tm, d), lambda i, j: (i, j, 0))
    in_specs = [spec_main]
    args = [h]
    if grid_rows:
        r = tm // GRID_W
        n_rows = s // GRID_W
        in_specs += [
            pl.BlockSpec((None, GRID_W, d), lambda i, j: (i, jnp.maximum(j * r - 1, 0), 0)),
            pl.BlockSpec((None, GRID_W, d), lambda i, j: (i, jnp.minimum((j + 1) * r, n_rows - 1), 0)),
        ]
        args += [h, h]
    in_specs += [
        pl.BlockSpec((None, N_MOD, d), lambda i, j: (row_of_batch(i), 0, 0)),
        _const_spec((1, d), lambda i, j: (0, 0)),
        _const_spec((d, 3 * d), lambda i, j: (0, 0)),
        _const_spec((3, d), lambda i, j: (0, 0)),
        _const_spec((d, d), lambda i, j: (0, 0)),
    ]
    args += [mods, g.reshape(1, d), win, cw, wout]
    return pl.pallas_call(
        functools.partial(_conv_kernel, period=GRID_W if grid_rows else s, grid_rows=grid_rows),
        grid=(b, s // tm),
        in_specs=in_specs,
        out_specs=spec_main,
        out_shape=jax.ShapeDtypeStruct(h.shape, _F32),
        compiler_params=_params(),
        name="conv_mixer",
    )(*args)


def _lower_bound(lb_ref, layer, direction):
    depth = lb_ref.shape[0]
    rows = [lb_ref[l][direction:direction + 1, :] for l in range(depth)]
    mx = functools.reduce(jnp.maximum, rows)
    ex = [jnp.exp(r - mx) for r in rows]
    den = functools.reduce(lambda u, v: u + v, ex)
    p = [e / den for e in ex]
    return functools.reduce(lambda u, v: u + v, p[:layer + 1]) - p[0]


def _chunk_sum_matrix(n, reverse):
    t = lax.broadcasted_iota(jnp.int32, (n, n), 0)
    s = lax.broadcasted_iota(jnp.int32, (n, n), 1)
    same_chunk = (t // HG_CHUNK) == (s // HG_CHUNK)
    return jnp.where(same_chunk & ((s >= t) if reverse else (s <= t)), 1.0, 0.0).astype(_BF16)


def _chunk_cumsum(x, tri):
    hi = x.astype(_BF16)
    r1 = x - hi.astype(_F32)
    mid = r1.astype(_BF16)
    lo = (r1 - mid.astype(_F32)).astype(_BF16)
    return _dot(tri, hi) + _dot(tri, mid) + _dot(tri, lo)


def _scan_chunks(qt_ref, kt_ref, kd_ref, dec_ref, v_ref, s_ref, o_ref, chunks, reverse):
    dk = qt_ref.shape[1] // HG_HEADS
    ti = lax.broadcasted_iota(jnp.int32, (HG_CHUNK, HG_CHUNK), 0)
    si = lax.broadcasted_iota(jnp.int32, (HG_CHUNK, HG_CHUNK), 1)
    mask = (ti <= si) if reverse else (ti >= si)
    tiles = [(c, h) for c in chunks for h in range(HG_HEADS)]

    def piece(ref, t):
        c, h = t
        return ref[c * HG_CHUNK:(c + 1) * HG_CHUNK, h * dk:(h + 1) * dk]

    att = {t: lax.dot_general(piece(qt_ref, t), piece(kt_ref, t), _NT, preferred_element_type=_F32)
           for t in tiles}
    upd = {t: lax.dot_general(piece(v_ref, t), piece(kd_ref, t), _TN, preferred_element_type=_F32)
           for t in tiles}
    intra = {t: _dot(jnp.where(mask, att[t], 0.0).astype(_BF16), piece(v_ref, t)) for t in tiles}
    for c in chunks:
        decay = dec_ref[c:c + 1, :]
        for h in range(HG_HEADS):
            state = s_ref[h]
            inter = lax.dot_general(piece(qt_ref, (c, h)), state.astype(_BF16), _NT,
                                    preferred_element_type=_F32)
            o_ref[c * HG_CHUNK:(c + 1) * HG_CHUNK, h * dk:(h + 1) * dk] = intra[(c, h)] + inter
            s_ref[h] = state * decay[:, h * dk:(h + 1) * dk] + upd[(c, h)]


def _hgrn_proj_kernel(h_ref, m_ref, g_ref, wq_ref, wi_ref, wff_ref, wfb_ref, wog_ref, lb_ref, s0_ref,
                      v_ref, sg_ref, ofw_ref, sfin_ref, qtb_ref, ktb_ref, kdb_ref, decb_ref,
                      qtf_ref, ktf_ref, kdf_ref, decf_ref, s_ref, *, layer):
    @pl.when(pl.program_id(1) == 0)
    def _():
        s_ref[...] = s0_ref[...]

    m = m_ref[...]
    g = g_ref[...]
    directions = ((_lower_bound(lb_ref, layer, 0), _chunk_sum_matrix(TM_SUB, False),
                   qtf_ref, ktf_ref, kdf_ref, decf_ref, False),
                  (_lower_bound(lb_ref, layer, 1), _chunk_sum_matrix(TM_SUB, True),
                   qtb_ref, ktb_ref, kdb_ref, decb_ref, True))
    chunks_per_sub = TM_SUB // HG_CHUNK
    n_sub = h_ref.shape[0] // TM_SUB

    def project(r):
        rows = slice(r * TM_SUB, (r + 1) * TM_SUB)
        a = _modulated(h_ref[rows, :], g, m, 3).astype(_BF16)
        q = _dot(a, wq_ref[...])
        logits = (_dot(a, wff_ref[...]), _dot(a, wfb_ref[...]))
        v_ref[rows, :] = _dot(a, wi_ref[...]).astype(_BF16)
        sg_ref[rows, :] = _silu(_dot(a, wog_ref[...]))
        return q, logits

    def scale(r, q, logits):
        rows = slice(r * TM_SUB, (r + 1) * TM_SUB)
        q = _silu(q)
        for logit, (lb, tri, qt_ref, kt_ref, kd_ref, dec_ref, reverse) in zip(logits, directions):
            f = lb + (1.0 - lb) * jax.nn.sigmoid(logit)
            gc = _chunk_cumsum(jnp.log(f), tri)
            kt = (1.0 - f) * jnp.exp(-gc)
            qt_ref[rows, :] = (q * jnp.exp(gc)).astype(_BF16)
            kt_ref[rows, :] = kt.astype(_BF16)
            edge = 0 if reverse else HG_CHUNK - 1
            for c in range(chunks_per_sub):
                lo = c * HG_CHUNK
                decay = jnp.exp(gc[lo + edge:lo + edge + 1, :])
                kd_ref[r * TM_SUB + lo:r * TM_SUB + lo + HG_CHUNK, :] = (kt[lo:lo + HG_CHUNK] * decay).astype(_BF16)
                row = r * chunks_per_sub + c
                dec_ref[row:row + 1, :] = decay

    pending = project(0)
    for r in range(n_sub):
        current = pending
        if r + 1 < n_sub:
            pending = project(r + 1)
        scale(r, *current)
    for r in range(n_sub):
        chunks = list(range(r * chunks_per_sub, (r + 1) * chunks_per_sub))
        _scan_chunks(qtf_ref, ktf_ref, kdf_ref, decf_ref, v_ref, s_ref, ofw_ref, chunks, False)
    sfin_ref[...] = s_ref[...]


def _hgrn_project(h, mods, row_of_batch, g, win, lb_logits, layer, s0, tm):
    b, s, d = h.shape
    tm = min(tm, s)
    depth = lb_logits.shape[0]
    tile = pl.BlockSpec((None, tm, d), lambda i, j: (i, j, 0))
    dec_tile = pl.BlockSpec((None, tm // HG_CHUNK, d), lambda i, j: (i, j, 0))
    state_shape = s0.shape[1:]
    state = pl.BlockSpec((None,) + state_shape, lambda i, j: (i, 0, 0, 0))
    wcol = lambda k: _const_spec((d, d), lambda i, j: (0, k))
    f32 = jax.ShapeDtypeStruct(h.shape, _F32)
    bf16 = jax.ShapeDtypeStruct(h.shape, _BF16)
    dec = jax.ShapeDtypeStruct((b, s // HG_CHUNK, d), _F32)
    v, sg, o_fw, s_fw, qtb, ktb, kdb, decb = pl.pallas_call(
        functools.partial(_hgrn_proj_kernel, layer=layer),
        grid=(b, s // tm),
        in_specs=[
            tile,
            pl.BlockSpec((None, N_MOD, d), lambda i, j: (row_of_batch(i), 0, 0)),
            _const_spec((1, d), lambda i, j: (0, 0)),
            wcol(0), wcol(1), wcol(2), wcol(3), wcol(4),
            _const_spec((depth, 2, d), lambda i, j: (0, 0, 0)),
            state,
        ],
        out_specs=[tile, tile, tile, state, tile, tile, tile, dec_tile],
        out_shape=[bf16, f32, f32, jax.ShapeDtypeStruct(s0.shape, _F32), bf16, bf16, bf16, dec],
        scratch_shapes=[pltpu.VMEM((tm, d), _BF16), pltpu.VMEM((tm, d), _BF16), pltpu.VMEM((tm, d), _BF16),
                        pltpu.VMEM((tm // HG_CHUNK, d), _F32), pltpu.VMEM(state_shape, _F32)],
        compiler_params=_params(),
        name="hgrn_proj",
    )(h, mods, g.reshape(1, d), win, win, win, win, win, lb_logits, s0)
    return v, sg, o_fw, s_fw, (qtb, ktb, kdb, decb)


def _hgrn_scan_kernel(*refs, reverse, finish):
    if finish:
        (qt_ref, kt_ref, kd_ref, dec_ref, v_ref, s0_ref, sg_ref, ofw_ref, h_ref, m_ref, gn_ref, wout_ref,
         out_ref, sfin_ref, s_ref, o_ref) = refs
    else:
        (qt_ref, kt_ref, kd_ref, dec_ref, v_ref, s0_ref, out_ref, sfin_ref, s_ref) = refs
        o_ref = out_ref
    tm, d = qt_ref.shape
    dk = d // HG_HEADS
    n_chunk = tm // HG_CHUNK

    @pl.when(pl.program_id(1) == 0)
    def _():
        s_ref[...] = s0_ref[...]

    order = [n_chunk - 1 - c for c in range(n_chunk)] if reverse else list(range(n_chunk))
    _scan_chunks(qt_ref, kt_ref, kd_ref, dec_ref, v_ref, s_ref, o_ref, order, reverse)
    sfin_ref[...] = s_ref[...]
    if finish:
        gn = gn_ref[...]
        gate = m_ref[...][5:6, :]
        for r in range(tm // TM_SUB):
            rows = slice(r * TM_SUB, (r + 1) * TM_SUB)
            o = ofw_ref[rows, :] + o_ref[rows, :]
            normed = [_rms_norm(o[:, h * dk:(h + 1) * dk], gn[:, h * dk:(h + 1) * dk]) for h in range(HG_HEADS)]
            gated = (jnp.concatenate(normed, axis=1) * sg_ref[rows, :]).astype(_BF16)
            out_ref[rows, :] = h_ref[rows, :] + gate * _dot(gated, wout_ref[...])


def _hgrn_scan(scaled, v, s0, reverse, tm, epilogue=None):
    qt, kt, kd, dec = scaled
    b, s, d = qt.shape
    tm = min(tm, s)
    n = s // tm
    blk = (lambda j: n - 1 - j) if reverse else (lambda j: j)
    tile = pl.BlockSpec((None, tm, d), lambda i, j: (i, blk(j), 0))
    state_shape = (HG_HEADS, d // HG_HEADS, d // HG_HEADS)
    state = pl.BlockSpec((None,) + state_shape, lambda i, j: (i, 0, 0, 0))
    in_specs = [tile, tile, tile, pl.BlockSpec((None, tm // HG_CHUNK, d), lambda i, j: (i, blk(j), 0)), tile, state]
    args = [qt, kt, kd, dec, v, s0]
    scratch = [pltpu.VMEM(state_shape, _F32)]
    if epilogue is not None:
        sg, o_fw, h, mods, gn, wout = epilogue
        in_specs += [tile, tile, tile,
                     pl.BlockSpec((None, N_MOD, d), lambda i, j: (i, 0, 0)),
                     _const_spec((1, d), lambda i, j: (0, 0)),
                     _const_spec((d, d), lambda i, j: (0, 0))]
        args += [sg, o_fw, h, mods, gn.reshape(1, d), wout]
        scratch.append(pltpu.VMEM((tm, d), _F32))
    return pl.pallas_call(
        functools.partial(_hgrn_scan_kernel, reverse=reverse, finish=epilogue is not None),
        grid=(b, n),
        in_specs=in_specs,
        out_specs=[tile, state],
        out_shape=[jax.ShapeDtypeStruct(qt.shape, _F32), jax.ShapeDtypeStruct((b,) + state_shape, _F32)],
        scratch_shapes=scratch,
        compiler_params=_params(),
        name="hgrn_bwd" if reverse else "hgrn_fwd",
    )(*args)


def kernel(x, c, ctx, c_ctx, ada_w, ada_b, norm_g, ffn_w_gu, ffn_w_down, conv_w_in, conv_w, conv_w_out,
           hg_w_in, hg_lb_logits, hg_gnorm_g, hg_w_out, final_norm_g):
    depth = ada_w.shape[0]
    b, _, d = x.shape
    assert depth == 2 and b < COND_ROWS
    cc = jnp.zeros((COND_ROWS, d), _F32).at[:b].set(c).at[b].set(c_ctx)
    mods = _ada_mods(cc, ada_w, ada_b)
    lat_row = lambda i: i
    ctx_row = lambda i: b
    wgu = _chunked_gate_up(ffn_w_gu)
    wdn = ffn_w_down.astype(_BF16).reshape(depth, 2, wgu.shape[2], FFN_CHUNK, d)

    h, hc = x, ctx
    m0 = mods[0]
    win = conv_w_in[0].astype(_BF16)
    wout = conv_w_out[0].astype(_BF16)
    h = _ffn_half(h, m0, lat_row, norm_g[0, 0], wgu, wdn, 0, 0, TM_FFN)
    hc = _ffn_half(hc, m0, ctx_row, norm_g[0, 0], wgu, wdn, 0, 0, TM_FFN)
    h = _conv_mixer(h, m0, lat_row, norm_g[0, 1], win, conv_w[0], wout, TM_CONV, True)
    hc = _conv_mixer(hc, m0, ctx_row, norm_g[0, 1], win, conv_w[0], wout, TM_CONV, False)
    h = _ffn_half(h, m0, lat_row, norm_g[0, 2], wgu, wdn, 0, 1, TM_FFN)
    hc = _ffn_half(hc, m0, ctx_row, norm_g[0, 2], wgu, wdn, 0, 1, TM_FFN)

    m1 = mods[1]
    hwin = hg_w_in[0].astype(_BF16)
    h = _ffn_half(h, m1, lat_row, norm_g[1, 0], wgu, wdn, 1, 0, TM_FFN)
    hc = _ffn_half(hc, m1, ctx_row, norm_g[1, 0], wgu, wdn, 1, 0, TM_FFN)
    dk = d // HG_HEADS
    s_zero = jnp.zeros((b, HG_HEADS, dk, dk), _F32)
    vc, _, _, s_fw, bwd_c = _hgrn_project(hc, m1, ctx_row, norm_g[1, 1], hwin, hg_lb_logits, 1, s_zero, TM_PROJ)
    v, sg, o_fw, _, bwd = _hgrn_project(h, m1, lat_row, norm_g[1, 1], hwin, hg_lb_logits, 1, s_fw, TM_PROJ)
    _, s_bw = _hgrn_scan(bwd_c, vc, s_zero, True, TM_SCAN)
    h, _ = _hgrn_scan(bwd, v, s_bw, True, TM_SCAN,
                      epilogue=(sg, o_fw, h, m1, hg_gnorm_g[0], hg_w_out[0].astype(_BF16)))
    return _ffn_half(h, m1, lat_row, norm_g[1, 2], wgu, wdn, 1, 1, TM_FFN, final_g=final_norm_g)
```

```python
import functools

import jax
import jax.numpy as jnp
from jax import lax
from jax.experimental import pallas as pl
from jax.experimental.pallas import tpu as pltpu

EPS = 1e-6
GRID_W = 64
N_MOD = 9
HG_HEADS = 8
HG_CHUNK = 64
COND_ROWS = 16
SUBLANES = 8

FFN_CHUNK = 256
TM_FFN = 1024
TM_PROJ = 512
TM_SUB = 256
TM_CONV = 1024
TM_SCAN = 512
VMEM_LIMIT = 56 * 1024 * 1024

_F32 = jnp.float32
_BF16 = jnp.bfloat16
_NT = (((1,), (1,)), ((), ()))
_TN = (((0,), (0,)), ((), ()))


def _params(n_axes=2):
    return pltpu.CompilerParams(dimension_semantics=("arbitrary",) * n_axes,
                                vmem_limit_bytes=VMEM_LIMIT)


def _const_spec(shape, index_map):
    return pl.BlockSpec(shape, index_map, pipeline_mode=pl.Buffered(1))


def _rms_norm(x, g):
    return x * lax.rsqrt(jnp.mean(x * x, axis=-1, keepdims=True) + EPS) * g


def _modulated(x, g, m, idx):
    return _rms_norm(x, g) * (1.0 + m[idx + 1:idx + 2, :]) + m[idx:idx + 1, :]


def _silu(x):
    return x * jax.nn.sigmoid(x)


def _dot(a, b):
    return jnp.dot(a, b, preferred_element_type=_F32)


def _ada_kernel(c_ref, w_ref, b_ref, o_ref):
    s = _silu(c_ref[...]).astype(_BF16)
    o_ref[...] = _dot(s, w_ref[...].astype(_BF16)) + b_ref[...]


def _ada_mods(cc, ada_w, ada_b):
    depth, d, nd = ada_w.shape
    out = pl.pallas_call(
        _ada_kernel,
        grid=(depth, nd // d),
        in_specs=[
            pl.BlockSpec((COND_ROWS, d), lambda l, k: (0, 0)),
            pl.BlockSpec((None, d, d), lambda l, k: (l, 0, k)),
            pl.BlockSpec((None, 1, d), lambda l, k: (l, 0, k)),
        ],
        out_specs=pl.BlockSpec((None, COND_ROWS, d), lambda l, k: (l, 0, k)),
        out_shape=jax.ShapeDtypeStruct((depth, COND_ROWS, nd), _F32),
        compiler_params=_params(),
        name="ada_mods",
    )(cc, ada_w, ada_b.reshape(depth, 1, nd))
    return out.reshape(depth, COND_ROWS, nd // d, d)


def _gate_up_kernel(g_ref, u_ref, o_ref):
    fc = g_ref.shape[1]
    o_ref[:, :fc] = g_ref[...].astype(_BF16)
    o_ref[:, fc:] = u_ref[...].astype(_BF16)


def _chunked_gate_up(w_gu):
    depth, two, d, f2 = w_gu.shape
    n = f2 // 2 // FFN_CHUNK
    return pl.pallas_call(
        _gate_up_kernel,
        grid=(depth, two, n),
        in_specs=[
            pl.BlockSpec((None, None, d, FFN_CHUNK), lambda l, s, c: (l, s, 0, c)),
            pl.BlockSpec((None, None, d, FFN_CHUNK), lambda l, s, c: (l, s, 0, n + c)),
        ],
        out_specs=pl.BlockSpec((None, None, None, d, 2 * FFN_CHUNK), lambda l, s, c: (l, s, c, 0, 0)),
        out_shape=jax.ShapeDtypeStruct((depth, two, n, d, 2 * FFN_CHUNK), _BF16),
        compiler_params=_params(3),
        name="chunk_gate_up",
    )(w_gu, w_gu)


def _ffn_kernel(*refs, idx, n_chunk, final):
    if final:
        h_ref, m_ref, g_ref, wgu_ref, wdn_ref, fg_ref, o_ref = refs
    else:
        h_ref, m_ref, g_ref, wgu_ref, wdn_ref, o_ref = refs
    m = m_ref[...]
    g = g_ref[...]
    fc = wdn_ref.shape[1]
    n_sub = h_ref.shape[0] // TM_SUB
    items = [(r, c) for r in range(n_sub) for c in range(n_chunk)]
    xs, acts, accs = {}, {}, {}

    def gate_up(i):
        r, c = items[i]
        if c == 0:
            xs[r] = h_ref[r * TM_SUB:(r + 1) * TM_SUB, :]
            acts[r] = _modulated(xs[r], g, m, idx).astype(_BF16)
            accs[r] = jnp.zeros(xs[r].shape, _F32)
        return _dot(acts[r], wgu_ref[c])

    gu_next = gate_up(0)
    for i, (r, c) in enumerate(items):
        gu = gu_next
        if i + 1 < len(items):
            gu_next = gate_up(i + 1)
        act = (_silu(gu[:, :fc]) * gu[:, fc:]).astype(_BF16)
        accs[r] = accs[r] + _dot(act, wdn_ref[c])
        if c == n_chunk - 1:
            out = xs[r] + 0.5 * m[idx + 2:idx + 3, :] * accs[r]
            if final:
                out = _rms_norm(out, fg_ref[...])
            o_ref[r * TM_SUB:(r + 1) * TM_SUB, :] = out


def _ffn_half(h, mods, row_of_batch, g, wgu, wdn, layer, which, tm, final_g=None):
    b, s, d = h.shape
    n_chunk, fc2 = wgu.shape[2], wgu.shape[4]
    tm = min(tm, s)
    idx = 6 * which
    in_specs = [
        pl.BlockSpec((None, tm, d), lambda i, j: (i, j, 0)),
        pl.BlockSpec((None, N_MOD, d), lambda i, j: (row_of_batch(i), 0, 0)),
        _const_spec((1, d), lambda i, j: (0, 0)),
        _const_spec((None, None, n_chunk, d, fc2), lambda i, j: (layer, which, 0, 0, 0)),
        _const_spec((None, None, n_chunk, fc2 // 2, d), lambda i, j: (layer, which, 0, 0, 0)),
    ]
    args = [h, mods, g.reshape(1, d), wgu, wdn]
    if final_g is not None:
        in_specs.append(_const_spec((1, d), lambda i, j: (0, 0)))
        args.append(final_g.reshape(1, d))
    return pl.pallas_call(
        functools.partial(_ffn_kernel, idx=idx, n_chunk=n_chunk, final=final_g is not None),
        grid=(b, s // tm),
        in_specs=in_specs,
        out_specs=pl.BlockSpec((None, tm, d), lambda i, j: (i, j, 0)),
        out_shape=jax.ShapeDtypeStruct(h.shape, _F32),
        compiler_params=_params(),
        name="ffn_half",
    )(*args)


def _conv_kernel(*refs, period, grid_rows):
    if grid_rows:
        h_ref, hp_ref, hn_ref, m_ref, g_ref, win_ref, cw_ref, wout_ref, o_ref = refs
    else:
        h_ref, m_ref, g_ref, win_ref, cw_ref, wout_ref, o_ref = refs
    x = h_ref[...]
    tm, d = x.shape
    half = d // 2
    m = m_ref[...]
    g = g_ref[...]
    cw = cw_ref[...]
    a = _modulated(x, g, m, 3).astype(_BF16)

    def seq_taps(zz, w):
        pos = lax.broadcasted_iota(jnp.int32, zz.shape, 0) % period
        left = jnp.where(pos == 0, 0.0, pltpu.roll(zz, 1, 0))
        right = jnp.where(pos == period - 1, 0.0, pltpu.roll(zz, zz.shape[0] - 1, 0))
        return left * w[0:1] + zz * w[1:2] + right * w[2:3]

    if grid_rows:
        j = pl.program_id(1)
        last = pl.num_programs(1) - 1
        ah = _modulated(jnp.concatenate([hp_ref[...], hn_ref[...]], axis=0), g, m, 3).astype(_BF16)
        zh = _dot(ah, win_ref[:, d + half:2 * d]) * _dot(ah, win_ref[:, 2 * d + half:])
    z = _dot(a, win_ref[:, d:2 * d]) * _dot(a, win_ref[:, 2 * d:])
    bg = _dot(a, win_ref[:, :d])
    if grid_rows:
        zv = z[:, half:]
        ext = jnp.concatenate([jnp.where(j > 0, zh[:GRID_W], 0.0), zv,
                               jnp.where(j < last, zh[GRID_W:], 0.0)], axis=0)
        wv = cw[:, half:]
        rows_v = ext[:tm] * wv[0:1] + zv * wv[1:2] + ext[2 * GRID_W:] * wv[2:3]
        zc = jnp.concatenate([seq_taps(z[:, :half], cw[:, :half]), rows_v], axis=1)
    else:
        zc = seq_taps(z, cw)
    y = _dot((bg * zc).astype(_BF16), wout_ref[...])
    o_ref[...] = x + m[5:6, :] * y


def _conv_mixer(h, mods, row_of_batch, g, win, cw, wout, tm, grid_rows):
    b, s, d = h.shape
    tm = min(tm, s)
    spec_main = pl.BlockSpec((None, tm, d), lambda i, j: (i, j, 0))
    in_specs = [spec_main]
    args = [h]
    if grid_rows:
        r = tm // GRID_W
        n_rows = s // GRID_W
        in_specs += [
            pl.BlockSpec((None, GRID_W, d), lambda i, j: (i, jnp.maximum(j * r - 1, 0), 0)),
            pl.BlockSpec((None, GRID_W, d), lambda i, j: (i, jnp.minimum((j + 1) * r, n_rows - 1), 0)),
        ]
        args += [h, h]
    in_specs += [
        pl.BlockSpec((None, N_MOD, d), lambda i, j: (row_of_batch(i), 0, 0)),
        _const_spec((1, d), lambda i, j: (0, 0)),
        _const_spec((d, 3 * d), lambda i, j: (0, 0)),
        _const_spec((3, d), lambda i, j: (0, 0)),
        _const_spec((d, d), lambda i, j: (0, 0)),
    ]
    args += [mods, g.reshape(1, d), win, cw, wout]
    return pl.pallas_call(
        functools.partial(_conv_kernel, period=GRID_W if grid_rows else s, grid_rows=grid_rows),
        grid=(b, s // tm),
        in_specs=in_specs,
        out_specs=spec_main,
        out_shape=jax.ShapeDtypeStruct(h.shape, _F32),
        compiler_params=_params(),
        name="conv_mixer",
    )(*args)


def _lower_bound(lb_ref, layer, direction):
    depth = lb_ref.shape[0]
    rows = [lb_ref[l][direction:direction + 1, :] for l in range(depth)]
    mx = functools.reduce(jnp.maximum, rows)
    ex = [jnp.exp(r - mx) for r in rows]
    den = functools.reduce(lambda u, v: u + v, ex)
    p = [e / den for e in ex]
    return functools.reduce(lambda u, v: u + v, p[:layer + 1]) - p[0]


def _chunk_sum_matrix(n, reverse):
    t = lax.broadcasted_iota(jnp.int32, (n, n), 0)
    s = lax.broadcasted_iota(jnp.int32, (n, n), 1)
    same_chunk = (t // HG_CHUNK) == (s // HG_CHUNK)
    return jnp.where(same_chunk & ((s >= t) if reverse else (s <= t)), 1.0, 0.0).astype(_BF16)


def _chunk_cumsum(x, tri):
    hi = x.astype(_BF16)
    r1 = x - hi.astype(_F32)
    mid = r1.astype(_BF16)
    lo = (r1 - mid.astype(_F32)).astype(_BF16)
    return _dot(tri, hi) + _dot(tri, mid) + _dot(tri, lo)


def _scan_chunks(qt_ref, kt_ref, kd_ref, dec_ref, v_ref, s_ref, o_ref, chunks, reverse):
    dk = qt_ref.shape[1] // HG_HEADS
    ti = lax.broadcasted_iota(jnp.int32, (HG_CHUNK, HG_CHUNK), 0)
    si = lax.broadcasted_iota(jnp.int32, (HG_CHUNK, HG_CHUNK), 1)
    mask = (ti <= si) if reverse else (ti >= si)
    tiles = [(c, h) for c in chunks for h in range(HG_HEADS)]

    def piece(ref, t):
        c, h = t
        return ref[c * HG_CHUNK:(c + 1) * HG_CHUNK, h * dk:(h + 1) * dk]

    att = {t: lax.dot_general(piece(qt_ref, t), piece(kt_ref, t), _NT, preferred_element_type=_F32)
           for t in tiles}
    upd = {t: lax.dot_general(piece(v_ref, t), piece(kd_ref, t), _TN, preferred_element_type=_F32)
           for t in tiles}
    intra = {t: _dot(jnp.where(mask, att[t], 0.0).astype(_BF16), piece(v_ref, t)) for t in tiles}
    for c in chunks:
        decay = dec_ref[c:c + 1, :]
        for h in range(HG_HEADS):
            state = s_ref[h]
            inter = lax.dot_general(piece(qt_ref, (c, h)), state.astype(_BF16), _NT,
                                    preferred_element_type=_F32)
            o_ref[c * HG_CHUNK:(c + 1) * HG_CHUNK, h * dk:(h + 1) * dk] = intra[(c, h)] + inter
            s_ref[h] = state * decay[:, h * dk:(h + 1) * dk] + upd[(c, h)]


def _hgrn_proj_kernel(h_ref, m_ref, g_ref, wq_ref, wi_ref, wff_ref, wfb_ref, wog_ref, lb_ref, s0_ref,
                      v_ref, sg_ref, ofw_ref, sfin_ref, qtb_ref, ktb_ref, kdb_ref, decb_ref,
                      qtf_ref, ktf_ref, kdf_ref, decf_ref, s_ref, *, layer):
    @pl.when(pl.program_id(1) == 0)
    def _():
        s_ref[...] = s0_ref[...]

    m = m_ref[...]
    g = g_ref[...]
    directions = ((_lower_bound(lb_ref, layer, 0), _chunk_sum_matrix(TM_SUB, False),
                   qtf_ref, ktf_ref, kdf_ref, decf_ref, False),
                  (_lower_bound(lb_ref, layer, 1), _chunk_sum_matrix(TM_SUB, True),
                   qtb_ref, ktb_ref, kdb_ref, decb_ref, True))
    chunks_per_sub = TM_SUB // HG_CHUNK
    n_sub = h_ref.shape[0] // TM_SUB

    def project(r):
        rows = slice(r * TM_SUB, (r + 1) * TM_SUB)
        a = _modulated(h_ref[rows, :], g, m, 3).astype(_BF16)
        q = _dot(a, wq_ref[...])
        logits = (_dot(a, wff_ref[...]), _dot(a, wfb_ref[...]))
        v_ref[rows, :] = _dot(a, wi_ref[...]).astype(_BF16)
        sg_ref[rows, :] = _silu(_dot(a, wog_ref[...]))
        return q, logits

    def scale(r, q, logits):
        rows = slice(r * TM_SUB, (r + 1) * TM_SUB)
        q = _silu(q)
        for logit, (lb, tri, qt_ref, kt_ref, kd_ref, dec_ref, reverse) in zip(logits, directions):
            f = lb + (1.0 - lb) * jax.nn.sigmoid(logit)
            gc = _chunk_cumsum(jnp.log(f), tri)
            kt = (1.0 - f) * jnp.exp(-gc)
            qt_ref[rows, :] = (q * jnp.exp(gc)).astype(_BF16)
            kt_ref[rows, :] = kt.astype(_BF16)
            edge = 0 if reverse else HG_CHUNK - 1
            for c in range(chunks_per_sub):
                lo = c * HG_CHUNK
                decay = jnp.exp(gc[lo + edge:lo + edge + 1, :])
                kd_ref[r * TM_SUB + lo:r * TM_SUB + lo + HG_CHUNK, :] = (kt[lo:lo + HG_CHUNK] * decay).astype(_BF16)
                row = r * chunks_per_sub + c
                dec_ref[row:row + 1, :] = decay

    pending = project(0)
    for r in range(n_sub):
        current = pending
        if r + 1 < n_sub:
            pending = project(r + 1)
        scale(r, *current)
    for r in range(n_sub):
        chunks = list(range(r * chunks_per_sub, (r + 1) * chunks_per_sub))
        _scan_chunks(qtf_ref, ktf_ref, kdf_ref, decf_ref, v_ref, s_ref, ofw_ref, chunks, False)
    sfin_ref[...] = s_ref[...]


def _hgrn_project(h, mods, row_of_batch, g, win, lb_logits, layer, s0, tm):
    b, s, d = h.shape
    tm = min(tm, s)
    depth = lb_logits.shape[0]
    tile = pl.BlockSpec((None, tm, d), lambda i, j: (i, j, 0))
    dec_tile = pl.BlockSpec((None, tm // HG_CHUNK, d), lambda i, j: (i, j, 0))
    state_shape = s0.shape[1:]
    state = pl.BlockSpec((None,) + state_shape, lambda i, j: (i, 0, 0, 0))
    wcol = lambda k: _const_spec((d, d), lambda i, j: (0, k))
    f32 = jax.ShapeDtypeStruct(h.shape, _F32)
    bf16 = jax.ShapeDtypeStruct(h.shape, _BF16)
    dec = jax.ShapeDtypeStruct((b, s // HG_CHUNK, d), _F32)
    v, sg, o_fw, s_fw, qtb, ktb, kdb, decb = pl.pallas_call(
        functools.partial(_hgrn_proj_kernel, layer=layer),
        grid=(b, s // tm),
        in_specs=[
            tile,
            pl.BlockSpec((None, N_MOD, d), lambda i, j: (row_of_batch(i), 0, 0)),
            _const_spec((1, d), lambda i, j: (0, 0)),
            wcol(0), wcol(1), wcol(2), wcol(3), wcol(4),
            _const_spec((depth, 2, d), lambda i, j: (0, 0, 0)),
            state,
        ],
        out_specs=[tile, tile, tile, state, tile, tile, tile, dec_tile],
        out_shape=[bf16, f32, f32, jax.ShapeDtypeStruct(s0.shape, _F32), bf16, bf16, bf16, dec],
        scratch_shapes=[pltpu.VMEM((tm, d), _BF16), pltpu.VMEM((tm, d), _BF16), pltpu.VMEM((tm, d), _BF16),
                        pltpu.VMEM((tm // HG_CHUNK, d), _F32), pltpu.VMEM(state_shape, _F32)],
        compiler_params=_params(),
        name="hgrn_proj",
    )(h, mods, g.reshape(1, d), win, win, win, win, win, lb_logits, s0)
    return v, sg, o_fw, s_fw, (qtb, ktb, kdb, decb)


def _hgrn_scan_kernel(*refs, reverse, finish):
    if finish:
        (qt_ref, kt_ref, kd_ref, dec_ref, v_ref, s0_ref, sg_ref, ofw_ref, h_ref, m_ref, gn_ref, wout_ref,
         out_ref, sfin_ref, s_ref, o_ref) = refs
    else:
        (qt_ref, kt_ref, kd_ref, dec_ref, v_ref, s0_ref, out_ref, sfin_ref, s_ref) = refs
        o_ref = out_ref
    tm, d = qt_ref.shape
    dk = d // HG_HEADS
    n_chunk = tm // HG_CHUNK

    @pl.when(pl.program_id(1) == 0)
    def _():
        s_ref[...] = s0_ref[...]

    order = [n_chunk - 1 - c for c in range(n_chunk)] if reverse else list(range(n_chunk))
    _scan_chunks(qt_ref, kt_ref, kd_ref, dec_ref, v_ref, s_ref, o_ref, order, reverse)
    sfin_ref[...] = s_ref[...]
    if finish:
        gn = gn_ref[...]
        gate = m_ref[...][5:6, :]
        for r in range(tm // TM_SUB):
            rows = slice(r * TM_SUB, (r + 1) * TM_SUB)
            o = ofw_ref[rows, :] + o_ref[rows, :]
            normed = [_rms_norm(o[:, h * dk:(h + 1) * dk], gn[:, h * dk:(h + 1) * dk]) for h in range(HG_HEADS)]
            gated = (jnp.concatenate(normed, axis=1) * sg_ref[rows, :]).astype(_BF16)
            out_ref[rows, :] = h_ref[rows, :] + gate * _dot(gated, wout_ref[...])


def _hgrn_scan(scaled, v, s0, reverse, tm, epilogue=None):
    qt, kt, kd, dec = scaled
    b, s, d = qt.shape
    tm = min(tm, s)
    n = s // tm
    blk = (lambda j: n - 1 - j) if reverse else (lambda j: j)
    tile = pl.BlockSpec((None, tm, d), lambda i, j: (i, blk(j), 0))
    state_shape = (HG_HEADS, d // HG_HEADS, d // HG_HEADS)
    state = pl.BlockSpec((None,) + state_shape, lambda i, j: (i, 0, 0, 0))
    in_specs = [tile, tile, tile, pl.BlockSpec((None, tm // HG_CHUNK, d), lambda i, j: (i, blk(j), 0)), tile, state]
    args = [qt, kt, kd, dec, v, s0]
    scratch = [pltpu.VMEM(state_shape, _F32)]
    if epilogue is not None:
        sg, o_fw, h, mods, gn, wout = epilogue
        in_specs += [tile, tile, tile,
                     pl.BlockSpec((None, N_MOD, d), lambda i, j: (i, 0, 0)),
                     _const_spec((1, d), lambda i, j: (0, 0)),
                     _const_spec((d, d), lambda i, j: (0, 0))]
        args += [sg, o_fw, h, mods, gn.reshape(1, d), wout]
        scratch.append(pltpu.VMEM((tm, d), _F32))
    return pl.pallas_call(
        functools.partial(_hgrn_scan_kernel, reverse=reverse, finish=epilogue is not None),
        grid=(b, n),
        in_specs=in_specs,
        out_specs=[tile, state],
        out_shape=[jax.ShapeDtypeStruct(qt.shape, _F32), jax.ShapeDtypeStruct((b,) + state_shape, _F32)],
        scratch_shapes=scratch,
        compiler_params=_params(),
        name="hgrn_bwd" if reverse else "hgrn_fwd",
    )(*args)


def kernel(x, c, ctx, c_ctx, ada_w, ada_b, norm_g, ffn_w_gu, ffn_w_down, conv_w_in, conv_w, conv_w_out,
           hg_w_in, hg_lb_logits, hg_gnorm_g, hg_w_out, final_norm_g):
    depth = ada_w.shape[0]
    b, _, d = x.shape
    assert depth == 2 and b < COND_ROWS
    cc = jnp.zeros((COND_ROWS, d), _F32).at[:b].set(c).at[b].set(c_ctx)
    mods = _ada_mods(cc, ada_w, ada_b)
    lat_row = lambda i: i
    ctx_row = lambda i: b
    wgu = _chunked_gate_up(ffn_w_gu)
    wdn = ffn_w_down.astype(_BF16).reshape(depth, 2, wgu.shape[2], FFN_CHUNK, d)

    h, hc = x, ctx
    m0 = mods[0]
    win = conv_w_in[0].astype(_BF16)
    wout = conv_w_out[0].astype(_BF16)
    h = _ffn_half(h, m0, lat_row, norm_g[0, 0], wgu, wdn, 0, 0, TM_FFN)
    hc = _ffn_half(hc, m0, ctx_row, norm_g[0, 0], wgu, wdn, 0, 0, TM_FFN)
    h = _conv_mixer(h, m0, lat_row, norm_g[0, 1], win, conv_w[0], wout, TM_CONV, True)
    hc = _conv_mixer(hc, m0, ctx_row, norm_g[0, 1], win, conv_w[0], wout, TM_CONV, False)
    h = _ffn_half(h, m0, lat_row, norm_g[0, 2], wgu, wdn, 0, 1, TM_FFN)
    hc = _ffn_half(hc, m0, ctx_row, norm_g[0, 2], wgu, wdn, 0, 1, TM_FFN)

    m1 = mods[1]
    hwin = hg_w_in[0].astype(_BF16)
    h = _ffn_half(h, m1, lat_row, norm_g[1, 0], wgu, wdn, 1, 0, TM_FFN)
    hc = _ffn_half(hc, m1, ctx_row, norm_g[1, 0], wgu, wdn, 1, 0, TM_FFN)
    dk = d // HG_HEADS
    s_zero = jnp.zeros((b, HG_HEADS, dk, dk), _F32)
    vc, _, _, s_fw, bwd_c = _hgrn_project(hc, m1, ctx_row, norm_g[1, 1], hwin, hg_lb_logits, 1, s_zero, TM_PROJ)
    v, sg, o_fw, _, bwd = _hgrn_project(h, m1, lat_row, norm_g[1, 1], hwin, hg_lb_logits, 1, s_fw, TM_PROJ)
    _, s_bw = _hgrn_scan(bwd_c, vc, s_zero, True, TM_SCAN)
    h, _ = _hgrn_scan(bwd, v, s_bw, True, TM_SCAN,
                      epilogue=(sg, o_fw, h, m1, hg_gnorm_g[0], hg_w_out[0].astype(_BF16)))
    return _ffn_half(h, m1, lat_row, norm_g[1, 2], wgu, wdn, 1, 1, TM_FFN, final_g=final_norm_g)
```
